```python
import math
import jax
import jax.numpy as jnp
from jax import lax
import numpy as np

D_MODEL = 2048
BATCH = 4
SEQ = 4096
DEPTH = 2

MIX_W = D_MODEL
POOL_W = MIX_W // 4
POOL_WINDOWS = (2, 4, 8, 16)
POOL_GROUP = POOL_W // len(POOL_WINDOWS)
GLA_W = MIX_W // 4
GLA_HEADS = 4
GLA_DV = GLA_W // GLA_HEADS
GLA_DK = GLA_DV // 2
GLA_RANK = 16
GLA_GATE_NORM = 16.0
GLA_CHUNK = 64
NSA_W = MIX_W // 2
NSA_HEAD_DIM = 64
NSA_HEADS = NSA_W // NSA_HEAD_DIM
NSA_GROUPS = 4
NSA_KV_W = NSA_GROUPS * NSA_HEAD_DIM
CMP_BLOCK = 32
CMP_STRIDE = 16
CMP_HIDDEN = 256
SLC_BLOCK = 64
N_SELECT = 16
WINDOW = 512
NSA_QBLOCK = 32
FORCE_BONUS = 100.0
FFN_HIDDEN = -(-8 * D_MODEL // (3 * 256)) * 256
EPS = 1e-6

IN_SIZES = (
    POOL_W,
    GLA_HEADS * GLA_DK,
    GLA_HEADS * GLA_DK,
    GLA_W,
    GLA_W,
    GLA_RANK,
    NSA_W,
    NSA_KV_W, NSA_KV_W,
    NSA_KV_W, NSA_KV_W,
    NSA_KV_W, NSA_KV_W,
    NSA_HEADS * 3,
)
IN_W = sum(IN_SIZES)

kernel_name = 'hybrid_pool_gla_nsa_block'


def rms_norm(x, gain):
    xf = x.astype(jnp.float32)
    y = xf * lax.rsqrt(jnp.mean(xf * xf, axis=-1, keepdims=True) + EPS)
    return y.astype(x.dtype) * gain


def modulate(h, shift, scale):
    return h * (1.0 + scale[:, None, :]) + shift[:, None, :]


def masked_softmax(s, mask):
    s = jnp.where(mask, s.astype(jnp.float32), -jnp.inf)
    m = jnp.max(s, axis=-1, keepdims=True)
    m = jnp.where(jnp.isfinite(m), m, 0.0)
    e = jnp.exp(s - m)
    d = jnp.sum(e, axis=-1, keepdims=True)
    return e / jnp.where(d > 0, d, 1.0)


def alibi_slopes(n):
    return np.asarray([2.0 ** (-8.0 * (i + 1) / n) for i in range(n)], dtype=np.float32)


def pool_mixer(u, w_pool, scale):
    b, s, _ = u.shape
    ng = len(POOL_WINDOWS)
    ug = u.reshape(b, s, ng, POOL_GROUP)
    cs = jnp.cumsum(ug.astype(jnp.float32), axis=1)
    cs = jnp.concatenate([jnp.zeros_like(cs[:, :1]), cs], axis=1)
    t = jnp.arange(s)
    pooled = []
    for gi, w in enumerate(POOL_WINDOWS):
        lo = jnp.maximum(t + 1 - w, 0)
        win_sum = cs[:, 1:, gi] - cs[:, lo, gi]
        pooled.append(win_sum / (t + 1 - lo).astype(jnp.float32)[None, :, None])
    pooled = jnp.stack(pooled, axis=2)
    d = (pooled - ug.astype(jnp.float32)).astype(u.dtype)
    y = jnp.einsum('bsgc,gce->bsge', d, w_pool)
    return y.reshape(b, s, POOL_W) * scale


def gla_chunked(q, k, v, log_a):
    b, s, h, dk = q.shape
    dv = v.shape[-1]
    n = s // GLA_CHUNK

    def chunks(a):
        return jnp.moveaxis(a.astype(jnp.float32).reshape(b, n, GLA_CHUNK, h, a.shape[-1]), 1, 0)

    qc, kc, vc = chunks(q), chunks(k), chunks(v)
    bc = jnp.cumsum(chunks(log_a), axis=2)
    causal = jnp.tril(jnp.ones((GLA_CHUNK, GLA_CHUNK), dtype=bool))[None, :, :, None, None]

    def step(state, inp):
        qi, ki, vi, bi = inp
        inter = jnp.einsum('bchd,bhde->bche', qi * jnp.exp(bi), state)
        dec = jnp.exp(jnp.where(causal, bi[:, :, None] - bi[:, None, :], -jnp.inf))
        att = jnp.einsum('bihd,bjhd,bijhd->bhij', qi, ki, dec)
        intra = jnp.einsum('bhij,bjhe->bihe', att, vi)
        last = bi[:, -1]
        state = (jnp.exp(last)[..., None] * state
                 + jnp.einsum('bchd,bche->bhde', ki * jnp.exp(last[:, None] - bi), vi))
        return state, inter + intra

    state0 = jnp.zeros((b, h, dk, dv), jnp.float32)
    _, out = lax.scan(step, state0, (qc, kc, vc, bc))
    return jnp.moveaxis(out, 0, 1).reshape(b, s, h, dv).astype(v.dtype)


def nsa_attention(q, k_cmp_tok, v_cmp_tok, k_slc, v_slc, k_win, v_win, gates,
                  cmp_pos, cmp_w1, cmp_w2, k_cmp_gain):
    b, s, h, dk = q.shape
    g = NSA_GROUPS
    r = h // g
    slopes = jnp.asarray(alibi_slopes(h).reshape(g, r))

    n_cmp = (s - CMP_BLOCK) // CMP_STRIDE + 1
    cmp_start = np.arange(n_cmp) * CMP_STRIDE
    cmp_idx = cmp_start[:, None] + np.arange(CMP_BLOCK)[None, :]
    cmp_end = jnp.asarray(cmp_idx[:, -1])

    def compress(tok, pos, w1, w2):
        blk = tok[:, cmp_idx] + pos[:, None, :]
        flat = jnp.moveaxis(blk, 3, 2).reshape(b, n_cmp, g, CMP_BLOCK * dk)
        return jax.nn.silu(flat @ w1) @ w2

    k_cmp = rms_norm(compress(k_cmp_tok, cmp_pos[0], cmp_w1[0], cmp_w2[0]), k_cmp_gain)
    v_cmp = compress(v_cmp_tok, cmp_pos[1], cmp_w1[1], cmp_w2[1])

    n_slc = s // SLC_BLOCK
    n_pick = min(N_SELECT, n_slc)
    slc_start = np.arange(n_slc) * SLC_BLOCK
    overlap = np.clip(np.minimum(cmp_start[:, None] + CMP_BLOCK, slc_start[None, :] + SLC_BLOCK)
                      - np.maximum(cmp_start[:, None], slc_start[None, :]), 0, None)
    cmp_to_slc = jnp.asarray((overlap / CMP_STRIDE).astype(np.float32))
    ks_blocks = jnp.moveaxis(k_slc.reshape(b, n_slc, SLC_BLOCK, g, dk), 3, 1)
    vs_blocks = jnp.moveaxis(v_slc.reshape(b, n_slc, SLC_BLOCK, g, dk), 3, 1)
    blk_ids = jnp.arange(n_slc)
    gather_blocks = jax.vmap(jax.vmap(lambda blocks, ix: blocks[ix]))

    pad = ((0, 0), (WINDOW, 0), (0, 0), (0, 0))
    kw_pad = jnp.pad(k_win, pad)
    vw_pad = jnp.pad(v_win, pad)

    qg = q.reshape(b, s, g, r, dk)
    gg = gates.reshape(b, s, g, r, 3)
    nq = s // NSA_QBLOCK

    def block_fn(i):
        q0 = i * NSA_QBLOCK
        qb = lax.dynamic_slice_in_dim(qg, q0, NSA_QBLOCK, axis=1)
        gb = lax.dynamic_slice_in_dim(gg, q0, NSA_QBLOCK, axis=1)
        t = q0 + jnp.arange(NSA_QBLOCK)

        s_c = jnp.einsum('bqgrd,bngd->bgrqn', qb, k_cmp).astype(jnp.float32)
        dist_c = (t[:, None] - cmp_end[None, :]).astype(jnp.float32)
        p_c = masked_softmax(s_c - slopes[:, :, None, None] * dist_c, dist_c >= 0)
        o_c = jnp.einsum('bgrqn,bngd->bqgrd', p_c.astype(v_cmp.dtype), v_cmp)

        imp = jnp.einsum('bgrqn,nj->bgqj', p_c, cmp_to_slc)
        cur = t // SLC_BLOCK
        forced = ((blk_ids[None, :] == 0) | (blk_ids[None, :] == cur[:, None])
                  | (blk_ids[None, :] == cur[:, None] - 1))
        valid_blk = blk_ids[None, :] * SLC_BLOCK <= t[:, None]
        score = jnp.where(valid_blk, imp + FORCE_BONUS * forced.astype(jnp.float32), -jnp.inf)
        _, sel = lax.top_k(score, n_pick)
        gk = gather_blocks(ks_blocks, sel)
        gv = gather_blocks(vs_blocks, sel)
        nl = n_pick * SLC_BLOCK
        s_s = jnp.einsum('bqgrd,bgqnld->bgrqnl', qb, gk).reshape(b, g, r, NSA_QBLOCK, nl)
        pos_s = (sel[..., None] * SLC_BLOCK + jnp.arange(SLC_BLOCK)).reshape(b, g, NSA_QBLOCK, nl)
        dist_s = (t[None, None, :, None] - pos_s).astype(jnp.float32)[:, :, None]
        p_s = masked_softmax(s_s.astype(jnp.float32) - slopes[None, :, :, None, None] * dist_s,
                             dist_s >= 0)
        o_s = jnp.einsum('bgrqk,bgqkd->bqgrd', p_s.astype(gv.dtype),
                         gv.reshape(b, g, NSA_QBLOCK, nl, dk))

        kwb = lax.dynamic_slice_in_dim(kw_pad, q0, WINDOW + NSA_QBLOCK, axis=1)
        vwb = lax.dynamic_slice_in_dim(vw_pad, q0, WINDOW + NSA_QBLOCK, axis=1)
        s_pos = q0 - WINDOW + jnp.arange(WINDOW + NSA_QBLOCK)
        dist_w = t[:, None] - s_pos[None, :]
        valid_w = (dist_w >= 0) & (dist_w < WINDOW) & (s_pos[None, :] >= 0)
        s_w = jnp.einsum('bqgrd,bkgd->bgrqk', qb, kwb).astype(jnp.float32)
        p_w = masked_softmax(s_w - slopes[:, :, None, None] * dist_w.astype(jnp.float32), valid_w)
        o_w = jnp.einsum('bgrqk,bkgd->bqgrd', p_w.astype(vwb.dtype), vwb)

        o = gb[..., 0:1] * o_c + gb[..., 1:2] * o_s + gb[..., 2:3] * o_w
        return o.reshape(b, NSA_QBLOCK, h * dk)

    outs = lax.map(block_fn, jnp.arange(nq))
    return jnp.moveaxis(outs, 0, 1).reshape(b, s, h * dk)


def setup_inputs(seed: int = 0) -> dict:
    key = jax.random.key(seed)
    ks = jax.random.split(key, 20)
    f32 = jnp.float32
    nrm = lambda k, shape, scale: jax.random.normal(k, shape, f32) * scale
    L, D = DEPTH, D_MODEL
    return {
        'x': nrm(ks[0], (BATCH, SEQ, D), 1.0),
        'c': nrm(ks[1], (BATCH, D), 1.0),
        'norm_mix_gain': 1.0 + nrm(ks[2], (L, D), 0.05),
        'norm_ffn_gain': 1.0 + nrm(ks[3], (L, D), 0.05),
        'w_mod': nrm(ks[4], (L, D, 6 * D), 0.5 * D ** -0.5),
        'b_mod': nrm(ks[5], (L, 6 * D), 0.02),
        'w_in': nrm(ks[6], (L, D, IN_W), D ** -0.5),
        'w_out': nrm(ks[7], (L, MIX_W, D), MIX_W ** -0.5),
        'pool_w': nrm(ks[8], (L, len(POOL_WINDOWS), POOL_GROUP, POOL_GROUP), POOL_GROUP ** -0.5),
        'pool_scale': 1.0 + nrm(ks[9], (L, POOL_W), 0.1),
        'gla_w_gk_up': nrm(ks[10], (L, GLA_RANK, GLA_HEADS * GLA_DK), GLA_RANK ** -0.5),
        'gla_b_gk': nrm(ks[11], (L, GLA_HEADS * GLA_DK), 0.02),
        'gla_norm_gain': 1.0 + nrm(ks[12], (L, GLA_DV), 0.05),
        'nsa_q_gain': 1.0 + nrm(ks[13], (L, NSA_HEAD_DIM), 0.05),
        'nsa_k_gain': 1.0 + nrm(ks[14], (L, 3, NSA_HEAD_DIM), 0.05),
        'nsa_cmp_pos': nrm(ks[15], (L, 2, CMP_BLOCK, NSA_HEAD_DIM), 0.02),
        'nsa_cmp_w1': nrm(ks[16], (L, 2, CMP_BLOCK * NSA_HEAD_DIM, CMP_HIDDEN), (CMP_BLOCK * NSA_HEAD_DIM) ** -0.5),
        'nsa_cmp_w2': nrm(ks[17], (L, 2, CMP_HIDDEN, NSA_HEAD_DIM), CMP_HIDDEN ** -0.5),
        'w_ffn_in': nrm(ks[18], (L, D, 2 * FFN_HIDDEN), D ** -0.5),
        'w_ffn_out': nrm(ks[19], (L, FFN_HIDDEN, D), FFN_HIDDEN ** -0.5),
    }


def reference(x, c, norm_mix_gain, norm_ffn_gain, w_mod, b_mod, w_in, w_out, pool_w, pool_scale,
              gla_w_gk_up, gla_b_gk, gla_norm_gain, nsa_q_gain, nsa_k_gain, nsa_cmp_pos,
              nsa_cmp_w1, nsa_cmp_w2, w_ffn_in, w_ffn_out):
    b, s, _ = x.shape
    c_act = jax.nn.silu(c)
    split_at = [int(v) for v in np.cumsum(IN_SIZES)[:-1]]
    G = NSA_GROUPS
    heads = lambda a, n: a.reshape(b, s, n, NSA_HEAD_DIM)
    for l in range(DEPTH):
        mod = c_act @ w_mod[l] + b_mod[l]
        sh_a, sc_a, g_a, sh_f, sc_f, g_f = jnp.split(mod, 6, axis=-1)

        hmix = modulate(rms_norm(x, norm_mix_gain[l]), sh_a, sc_a)
        proj = hmix @ w_in[l]
        (u, gla_q, gla_k, gla_v, gla_g, gla_low, nsa_q, kc, vc, ksl, vsl, kwn, vwn,
         nsa_g) = jnp.split(proj, split_at, axis=-1)

        pool_out = pool_mixer(u, pool_w[l], pool_scale[l])

        log_a = jax.nn.log_sigmoid((gla_low @ gla_w_gk_up[l] + gla_b_gk[l]).astype(jnp.float32)) / GLA_GATE_NORM
        o_gla = gla_chunked(gla_q.reshape(b, s, GLA_HEADS, GLA_DK) * GLA_DK ** -0.5,
                            gla_k.reshape(b, s, GLA_HEADS, GLA_DK),
                            gla_v.reshape(b, s, GLA_HEADS, GLA_DV),
                            log_a.reshape(b, s, GLA_HEADS, GLA_DK))
        gla_out = rms_norm(o_gla, gla_norm_gain[l]).reshape(b, s, GLA_W) * jax.nn.silu(gla_g)

        kg = nsa_k_gain[l]
        q = rms_norm(heads(nsa_q, NSA_HEADS), nsa_q_gain[l]) * NSA_HEAD_DIM ** -0.5
        nsa_out = nsa_attention(q, heads(kc, G), heads(vc, G),
                                rms_norm(heads(ksl, G), kg[1]), heads(vsl, G),
                                rms_norm(heads(kwn, G), kg[2]), heads(vwn, G),
                                jax.nn.sigmoid(nsa_g).reshape(b, s, NSA_HEADS, 3),
                                nsa_cmp_pos[l], nsa_cmp_w1[l], nsa_cmp_w2[l], kg[0])

        mix = jnp.concatenate([pool_out, gla_out, nsa_out], axis=-1) @ w_out[l]
        x = x + g_a[:, None, :] * mix

        hf = modulate(rms_norm(x, norm_ffn_gain[l]), sh_f, sc_f)
        gate_h, up_h = jnp.split(hf @ w_ffn_in[l], 2, axis=-1)
        x = x + g_f[:, None, :] * ((jax.nn.silu(gate_h) * up_h) @ w_ffn_out[l])
    return x
```

```python
import functools
import math

import jax
import jax.numpy as jnp
import numpy as np
from jax import lax
from jax.experimental import pallas as pl
from jax.experimental.pallas import tpu as pltpu

F32 = jnp.float32
BF16 = jnp.bfloat16

POOL_WINDOWS = (2, 4, 8, 16)
POOL_GROUP = 128
GLA_HEADS = 4
GLA_DK = 64
GLA_DV = 128
GLA_RANK = 16
GLA_GATE_NORM = 16.0
NSA_HEADS = 16
NSA_GROUPS = 4
NSA_REP = NSA_HEADS // NSA_GROUPS
NSA_DK = 64
CMP_BLOCK = 32
CMP_STRIDE = 16
SLC_BLOCK = 64
N_SELECT = 16
WINDOW = 512
FORCE_BONUS = 100.0
EPS = 1e-6

LANES = 128
VMEM_LIMIT = 56 * 1024 * 1024

GLA_SUB = 16
GLA_ROWS = 256
NSA_TQ = 128
NSA_TKS = 512
NSA_TKW = 128
MASK_BIG = 2.0 ** 100

_NT = (((1,), (1,)), ((), ()))
_TN = (((0,), (0,)), ((), ()))


def _cparams(*sem):
    return pltpu.CompilerParams(dimension_semantics=sem, vmem_limit_bytes=VMEM_LIMIT)


def _dot(a, b):
    return jnp.dot(a, b, preferred_element_type=F32)


def _split3(x):
    hi = x.astype(BF16)
    r1 = x - hi.astype(F32)
    mid = r1.astype(BF16)
    lo = (r1 - mid.astype(F32)).astype(BF16)
    return hi, mid, lo


def _silu(x):
    return x * jax.nn.sigmoid(x)


def _mod_kernel(ct_ref, w_ref, b_ref, o_ref):
    ct = ct_ref[...]
    act = _silu(ct)
    w = w_ref[0]
    for b in range(ct.shape[1]):
        row = jnp.sum(act[:, b:b + 1] * w, axis=0, keepdims=True)
        o_ref[0, b:b + 1, :] = row + b_ref[0]


def _mod_call(c, w_mod, b_mod, tn=512):
    nl, d, n = w_mod.shape
    nb = c.shape[0]
    return pl.pallas_call(
        _mod_kernel,
        grid=(nl, n // tn),
        in_specs=[pl.BlockSpec((d, nb), lambda l, j: (0, 0)),
                  pl.BlockSpec((1, d, tn), lambda l, j: (l, 0, j)),
                  pl.BlockSpec((1, 1, tn), lambda l, j: (l, 0, j))],
        out_specs=pl.BlockSpec((1, nb, tn), lambda l, j: (l, 0, j)),
        out_shape=jax.ShapeDtypeStruct((nl, nb, n), F32),
        compiler_params=_cparams("parallel", "parallel"),
        name="adaln_mod",
    )(c.T, w_mod, b_mod.reshape(nl, 1, n))


def _norm_mod_kernel(x_ref, g_ref, sh_ref, sc_ref, o_ref):
    x = x_ref[0]
    y = x * lax.rsqrt(jnp.mean(x * x, axis=-1, keepdims=True) + EPS) * g_ref[...]
    o_ref[0] = (y * (1.0 + sc_ref[0]) + sh_ref[0]).astype(o_ref.dtype)


def _norm_mod_call(x, gain, shift, scale, tm=512):
    nb, s, d = x.shape
    row = pl.BlockSpec((1, 1, d), lambda b, i: (b, 0, 0))
    return pl.pallas_call(
        _norm_mod_kernel,
        grid=(nb, s // tm),
        in_specs=[pl.BlockSpec((1, tm, d), lambda b, i: (b, i, 0)),
                  pl.BlockSpec((1, d), lambda b, i: (0, 0)),
                  row, row],
        out_specs=pl.BlockSpec((1, tm, d), lambda b, i: (b, i, 0)),
        out_shape=jax.ShapeDtypeStruct((nb, s, d), BF16),
        compiler_params=_cparams("parallel", "parallel"),
        name="norm_modulate",
    )(x, gain.reshape(1, d), shift.reshape(nb, 1, d), scale.reshape(nb, 1, d))


def _matmul_kernel(a_ref, w_ref, o_ref):
    o_ref[...] = _dot(a_ref[...], w_ref[...]).astype(o_ref.dtype)


def _matmul_call(a, w, out_dtype, tm, tn, name):
    m, k = a.shape
    n = w.shape[1]
    return pl.pallas_call(
        _matmul_kernel,
        grid=(n // tn, m // tm),
        in_specs=[pl.BlockSpec((tm, k), lambda j, i: (i, 0)),
                  pl.BlockSpec((k, tn), lambda j, i: (0, j))],
        out_specs=pl.BlockSpec((tm, tn), lambda j, i: (i, j)),
        out_shape=jax.ShapeDtypeStruct((m, n), out_dtype),
        compiler_params=_cparams("parallel", "parallel"),
        name=name,
    )(a, w)


def _pool_kernel(u_ref, w_ref, sc_ref, o_ref):
    s = u_ref.shape[1]
    t = lax.broadcasted_iota(jnp.int32, (s, 1), 0)
    for gi, win in enumerate(POOL_WINDOWS):
        cols = slice(gi * POOL_GROUP, (gi + 1) * POOL_GROUP)
        u = u_ref[0, :, cols].astype(F32)
        acc = u
        k = 1
        while k < win:
            acc = acc + jnp.where(t >= k, pltpu.roll(acc, k, axis=0), 0.0)
            k *= 2
        cnt = jnp.minimum(t + 1, win).astype(F32)
        dlt = (acc / cnt - u).astype(BF16)
        y = _dot(dlt, w_ref[gi])
        o_ref[0, :, cols] = (y * sc_ref[:, cols]).astype(o_ref.dtype)


def _pool_call(proj, pool_w, pool_scale):
    nb, s, _ = proj.shape
    pw = len(POOL_WINDOWS) * POOL_GROUP
    return pl.pallas_call(
        _pool_kernel,
        grid=(nb,),
        in_specs=[pl.BlockSpec((1, s, pw), lambda b: (b, 0, 0)),
                  pl.BlockSpec(pool_w.shape, lambda b: (0, 0, 0)),
                  pl.BlockSpec((1, pw), lambda b: (0, 0))],
        out_specs=pl.BlockSpec((1, s, pw), lambda b: (b, 0, 0)),
        out_shape=jax.ShapeDtypeStruct((nb, s, pw), BF16),
        compiler_params=_cparams("parallel"),
        name="pool_mixer",
    )(proj, pool_w.astype(BF16), pool_scale.reshape(1, pw))


def _gla_kernel(q_ref, k_ref, v_ref, g_ref, ps_ref, wup_ref, bgk_ref, gain_ref,
                tri_ref, ones_ref, seg_ref, o_ref,
                st_ref, qin_s, kdec_s, b_s, dl_s, qf_s, kf_s, o_s):
    rows = q_ref.shape[1]
    hk = GLA_HEADS * GLA_DK

    @pl.when(pl.program_id(1) == 0)
    def _():
        st_ref[...] = jnp.zeros_like(st_ref)

    low = ps_ref[0][:, :GLA_RANK]
    wup = wup_ref[...]
    x = jnp.broadcast_to(bgk_ref[...], (rows, hk))
    for r in range(GLA_RANK):
        x = x + low[:, r:r + 1] * wup[r:r + 1, :]
    la = (jnp.minimum(x, 0.0) - jnp.log1p(jnp.exp(-jnp.abs(x)))) * (1.0 / GLA_GATE_NORM)
    pieces = _split3(la)
    tri = tri_ref[...]
    ones = ones_ref[...]
    bcum = _dot(tri, pieces[0]) + _dot(tri, pieces[1]) + _dot(tri, pieces[2])
    tot = _dot(ones, pieces[0]) + _dot(ones, pieces[1]) + _dot(ones, pieces[2])
    q = q_ref[0].astype(F32) * (GLA_DK ** -0.5)
    k = k_ref[0].astype(F32)
    qf_s[...] = q
    kf_s[...] = k
    b_s[...] = bcum
    qin_s[...] = (q * jnp.exp(bcum)).astype(BF16)
    kdec_s[...] = (k * jnp.exp(tot - bcum)).astype(BF16)
    dl_s[...] = jnp.exp(tot)

    sub_i = lax.broadcasted_iota(jnp.int32, (GLA_SUB, 1), 0)
    seg = seg_ref[...]

    def step(n, carry):
        r0 = pl.multiple_of(n * GLA_SUB, GLA_SUB)
        rs = pl.ds(r0, GLA_SUB)
        bn = b_s[rs, :]
        qn = qf_s[rs, :]
        kn = kf_s[rs, :]
        vn = v_ref[0, rs, :]
        vf = vn.astype(F32)
        ws = []
        for j in range(GLA_SUB):
            e = jnp.exp(jnp.where(sub_i >= j, bn - bn[j:j + 1, :], -jnp.inf))
            ws.append((qn * (kn[j:j + 1, :] * e)).astype(BF16))
        z = _dot(jnp.concatenate(ws, axis=0), seg)
        intra = z[0:GLA_SUB, :] * vf[0:1, :]
        for j in range(1, GLA_SUB):
            intra = intra + z[j * GLA_SUB:(j + 1) * GLA_SUB, :] * vf[j:j + 1, :]
        qin = qin_s[rs, :]
        kdec = kdec_s[rs, :]
        dl = dl_s[pl.ds(r0, 1), :]
        st = st_ref[...]
        stb = st.astype(BF16)
        st_new = st * dl
        for h in range(GLA_HEADS):
            ck = slice(h * GLA_DK, (h + 1) * GLA_DK)
            cv = slice(h * GLA_DV, (h + 1) * GLA_DV)
            inter = lax.dot_general(qin[:, ck], stb[:, ck], _NT, preferred_element_type=F32)
            o_s[rs, cv] = inter + intra[:, cv]
            upd = lax.dot_general(vn[:, cv], kdec[:, ck], _TN, preferred_element_type=F32)
            st_ref[:, ck] = st_new[:, ck] + upd
        return carry

    lax.fori_loop(0, rows // GLA_SUB, step, 0)

    gain = gain_ref[...]
    for h in range(GLA_HEADS):
        cv = slice(h * GLA_DV, (h + 1) * GLA_DV)
        o = o_s[:, cv]
        y = o * lax.rsqrt(jnp.mean(o * o, axis=-1, keepdims=True) + EPS) * gain
        o_ref[0, :, cv] = (y * _silu(g_ref[0, :, cv].astype(F32))).astype(o_ref.dtype)


def _gla_call(proj, small, w_up, b_gk, gain):
    nb, s, _ = proj.shape
    rows = min(GLA_ROWS, s)
    hk, hv = GLA_HEADS * GLA_DK, GLA_HEADS * GLA_DV
    idx = np.arange(rows)
    same = (idx[:, None] // GLA_SUB) == (idx[None, :] // GLA_SUB)
    tri = jnp.asarray(same & (idx[None, :] <= idx[:, None]), BF16)
    ones = jnp.asarray(same, BF16)
    seg = jnp.asarray((np.arange(hk)[:, None] // GLA_DK) == (np.arange(hv)[None, :] // GLA_DV), BF16)
    const = lambda shape: pl.BlockSpec(shape, lambda b, i: (0,) * len(shape))
    return pl.pallas_call(
        _gla_kernel,
        grid=(nb, s // rows),
        in_specs=[pl.BlockSpec((1, rows, hk), lambda b, i: (b, i, 2)),
                  pl.BlockSpec((1, rows, hk), lambda b, i: (b, i, 3)),
                  pl.BlockSpec((1, rows, hv), lambda b, i: (b, i, 2)),
                  pl.BlockSpec((1, rows, hv), lambda b, i: (b, i, 3)),
                  pl.BlockSpec((1, rows, LANES), lambda b, i: (b, i, 0)),
                  const((GLA_RANK, hk)), const((1, hk)), const((1, GLA_DV)),
                  const((rows, rows)), const((rows, rows)), const((hk, hv))],
        out_specs=pl.BlockSpec((1, rows, hv), lambda b, i: (b, i, 0)),
        out_shape=jax.ShapeDtypeStruct((nb, s, hv), BF16),
        scratch_shapes=[pltpu.VMEM((GLA_DV, hk), F32),
                        pltpu.VMEM((rows, hk), BF16), pltpu.VMEM((rows, hk), BF16),
                        pltpu.VMEM((rows, hk), F32), pltpu.VMEM((rows, hk), F32),
                        pltpu.VMEM((rows, hk), F32), pltpu.VMEM((rows, hk), F32),
                        pltpu.VMEM((rows, hv), F32)],
        compiler_params=_cparams("parallel", "arbitrary"),
        name="gla",
    )(proj, proj, proj, proj, small, w_up, b_gk.reshape(1, hk), gain.reshape(1, GLA_DV), tri, ones, seg)


def _head_norm(x, gain):
    return x * lax.rsqrt(jnp.mean(x * x, axis=-1, keepdims=True) + EPS) * gain


def _nsa_prep_kernel(q_ref, ks_ref, vs_ref, kw_ref, vw_ref, ps_ref, qg_ref, kg_ref,
                     q2_ref, ksa_ref, kwp_ref, vsw_ref, gt_ref):
    ts = q_ref.shape[1]
    nblk = LANES - NSA_DK
    zeros = jnp.zeros((ts, NSA_DK), BF16)
    qg = qg_ref[...]
    kg = kg_ref[...]
    for h in range(NSA_HEADS):
        c = slice(h * NSA_DK, (h + 1) * NSA_DK)
        qn = _head_norm(q_ref[0, :, c].astype(F32), qg) * (NSA_DK ** -0.5)
        q2_ref[0, h // NSA_REP, h % NSA_REP, :, 0:NSA_DK] = qn.astype(BF16)
        q2_ref[0, h // NSA_REP, h % NSA_REP, :, NSA_DK:] = jnp.zeros((ts, q2_ref.shape[4] - NSA_DK), BF16)
    pos = pl.program_id(1) * ts + lax.broadcasted_iota(jnp.int32, (ts, nblk), 0)
    blk = lax.broadcasted_iota(jnp.int32, (ts, nblk), 1)
    onehot = (jnp.right_shift(pos, int(math.log2(SLC_BLOCK))) == blk).astype(BF16)
    gates = jax.nn.sigmoid(ps_ref[0])
    for g in range(NSA_GROUPS):
        c = slice(g * NSA_DK, (g + 1) * NSA_DK)
        ksa_ref[0, g, :, 0:NSA_DK] = _head_norm(ks_ref[0, :, c].astype(F32), kg[1:2, :]).astype(BF16)
        ksa_ref[0, g, :, NSA_DK:] = onehot
        kwp_ref[0, g, :, 0:NSA_DK] = _head_norm(kw_ref[0, :, c].astype(F32), kg[2:3, :]).astype(BF16)
        kwp_ref[0, g, :, NSA_DK:] = zeros
        vsw_ref[0, g, :, 0:NSA_DK] = vs_ref[0, :, c]
        vsw_ref[0, g, :, NSA_DK:] = vw_ref[0, :, c]
        gw = 3 * NSA_REP
        gt_ref[0, g, :, 0:gw] = gates[:, GLA_RANK + g * gw:GLA_RANK + (g + 1) * gw]
        gt_ref[0, g, :, gw:] = jnp.zeros((ts, LANES - gw), F32)


def _nsa_prep_call(proj, small, q_gain, k_gain, ts=512):
    nb, s, _ = proj.shape
    ts = min(ts, s)
    assert s // SLC_BLOCK <= LANES - NSA_DK
    kvw = NSA_GROUPS * NSA_DK
    qw = NSA_HEADS * NSA_DK
    col = lambda j: (lambda b, i: (b, i, j))
    grp = lambda w: pl.BlockSpec((1, NSA_GROUPS, ts, w), lambda b, i: (b, 0, i, 0))
    return pl.pallas_call(
        _nsa_prep_kernel,
        grid=(nb, s // ts),
        in_specs=[pl.BlockSpec((1, ts, qw), col(2)),
                  pl.BlockSpec((1, ts, kvw), col(14)),
                  pl.BlockSpec((1, ts, kvw), col(15)),
                  pl.BlockSpec((1, ts, kvw), col(16)),
                  pl.BlockSpec((1, ts, kvw), col(17)),
                  pl.BlockSpec((1, ts, LANES), col(0)),
                  pl.BlockSpec((1, NSA_DK), lambda b, i: (0, 0)),
                  pl.BlockSpec((3, NSA_DK), lambda b, i: (0, 0))],
        out_specs=[pl.BlockSpec((1, NSA_GROUPS, NSA_REP, ts, LANES), lambda b, i: (b, 0, 0, i, 0)),
                   grp(LANES), grp(LANES), grp(LANES), grp(LANES)],
        out_shape=[jax.ShapeDtypeStruct((nb, NSA_GROUPS, NSA_REP, s, LANES), BF16),
                   jax.ShapeDtypeStruct((nb, NSA_GROUPS, s, LANES), BF16),
                   jax.ShapeDtypeStruct((nb, NSA_GROUPS, s, LANES), BF16),
                   jax.ShapeDtypeStruct((nb, NSA_GROUPS, s, LANES), BF16),
                   jax.ShapeDtypeStruct((nb, NSA_GROUPS, s, LANES), F32)],
        compiler_params=_cparams("parallel", "parallel"),
        name="nsa_prep",
    )(proj, proj, proj, proj, proj, small, q_gain.reshape(1, NSA_DK), k_gain)


def _nsa_compress_kernel(hk_ref, hv_ref, pos_ref, w1_ref, w2_ref, kg_ref, kc_ref, vc_ref):
    nh = hk_ref.shape[2]
    half = CMP_STRIDE * NSA_DK
    row = lax.broadcasted_iota(jnp.int32, (nh, 1), 0)
    for kv, (h_ref, o_ref) in enumerate(((hk_ref, kc_ref), (hv_ref, vc_ref))):
        hb = h_ref[0, 0].astype(F32)
        top = _dot((hb + pos_ref[kv, 0:1, :]).astype(BF16), w1_ref[kv, 0:half, :])
        bot = _dot((hb + pos_ref[kv, 1:2, :]).astype(BF16), w1_ref[kv, half:, :])
        hid = _silu(top + pltpu.roll(bot, nh - 1, axis=0))
        out = _dot(hid.astype(BF16), w2_ref[kv])
        if kv == 0:
            out = _head_norm(out, kg_ref[...])
        out = jnp.where(row < nh - 1, out, 0.0)
        o_ref[0, 0, :, 0:NSA_DK] = out.astype(BF16)
        o_ref[0, 0, :, NSA_DK:] = jnp.zeros((nh, LANES - NSA_DK), BF16)


def _nsa_compress_call(hk, hv, pos, w1, w2, kgain0):
    nb, ng, nh, hw = hk.shape
    hid = w1.shape[2]
    blk = pl.BlockSpec((1, 1, nh, hw), lambda b, g: (b, g, 0, 0))
    out = pl.BlockSpec((1, 1, nh, LANES), lambda b, g: (b, g, 0, 0))
    return pl.pallas_call(
        _nsa_compress_kernel,
        grid=(nb, ng),
        in_specs=[blk, blk,
                  pl.BlockSpec((2, 2, hw), lambda b, g: (0, 0, 0)),
                  pl.BlockSpec((2, 2 * hw, hid), lambda b, g: (0, 0, 0)),
                  pl.BlockSpec((2, hid, NSA_DK), lambda b, g: (0, 0, 0)),
                  pl.BlockSpec((1, NSA_DK), lambda b, g: (0, 0))],
        out_specs=[out, out],
        out_shape=[jax.ShapeDtypeStruct((nb, ng, nh, LANES), BF16)] * 2,
        compiler_params=_cparams("parallel", "parallel"),
        name="nsa_compress",
    )(hk, hv, pos.reshape(2, 2, hw), w1.astype(BF16), w2.astype(BF16), kgain0.reshape(1, NSA_DK))


def _softmax_rows(s):
    m = jnp.max(s, axis=-1, keepdims=True)
    m = jnp.where(m > -jnp.inf, m, 0.0)
    e = jnp.exp(s - m)
    d = jnp.sum(e, axis=-1, keepdims=True)
    return e / jnp.where(d > 0, d, 1.0)


def _nsa_attn_kernel(slopes_ref, q_ref, kc_ref, vc_ref, ksa_ref, kwp_ref, vsw_ref, gt_ref,
                     mt_ref, psel_ref, o_ref):
    tq = q_ref.shape[3]
    r = NSA_REP * tq
    nc = kc_ref.shape[2]
    nblk = mt_ref.shape[0]
    g = pl.program_id(1)
    q0 = pl.program_id(2) * tq

    q2 = q_ref[0, 0].reshape(r, LANES)
    rowi = lax.broadcasted_iota(jnp.int32, (r, 1), 0)
    rep = jnp.zeros((r, 1), jnp.int32)
    for i in range(1, NSA_REP):
        rep = rep + (rowi >= i * tq).astype(jnp.int32)
    t_col = q0 + (rowi - rep * tq)
    slope = jnp.zeros((r, 1), F32)
    for i in range(NSA_REP):
        slope = jnp.where(rep == i, slopes_ref[g * NSA_REP + i], slope)

    s_c = lax.dot_general(q2, kc_ref[0, 0], _NT, preferred_element_type=F32)
    cend = lax.broadcasted_iota(jnp.int32, (1, nc), 1) * CMP_STRIDE + (CMP_BLOCK - 1)
    dist_c = t_col - cend
    p_c = _softmax_rows(jnp.where(dist_c >= 0, s_c - slope * dist_c.astype(F32), -jnp.inf))
    o_c = _dot(p_c.astype(BF16), vc_ref[0, 0])

    p_sum = p_c[0:tq]
    for i in range(1, NSA_REP):
        p_sum = p_sum + p_c[i * tq:(i + 1) * tq]
    mt = mt_ref[...]
    imp = sum(lax.dot_general(mt, piece, _NT, preferred_element_type=F32) for piece in _split3(p_sum))
    blk = lax.broadcasted_iota(jnp.int32, (nblk, 1), 0).astype(F32)
    t_row = q0 + lax.broadcasted_iota(jnp.int32, (1, tq), 1)
    cur = jnp.right_shift(t_row, int(math.log2(SLC_BLOCK))).astype(F32)
    forced = (blk == 0.0) | (blk == cur) | (blk == cur - 1.0)
    valid = blk <= cur
    score = jnp.where(valid, imp + FORCE_BONUS * forced.astype(F32), -jnp.inf)
    sel = jnp.zeros((nblk, tq), F32)
    for _ in range(min(N_SELECT, nblk)):
        best = jnp.max(score, axis=0, keepdims=True)
        first = jnp.min(jnp.where(score == best, blk, float(nblk)), axis=0, keepdims=True)
        pick = blk == first
        sel = jnp.where(pick, 1.0, sel)
        score = jnp.where(pick, -jnp.inf, score)
    sel_bias = jnp.where((sel > 0.0) & valid, 0.0, -MASK_BIG).astype(BF16)
    bias_q = lax.dot_general(sel_bias, psel_ref[...], _TN, preferred_element_type=F32)
    bias_q = bias_q.astype(BF16)
    q_sel = jnp.concatenate([q_ref[0, 0, i] + bias_q for i in range(NSA_REP)], axis=0)

    def attend(q, k_ref, lo, hi, tk, in_window):
        def body(j, carry):
            m, l, acc = carry
            k0 = pl.multiple_of(j * tk, tk)
            s = lax.dot_general(q, k_ref[0, 0, pl.ds(k0, tk), :], _NT, preferred_element_type=F32)
            dist = t_col - (k0 + lax.broadcasted_iota(jnp.int32, (1, tk), 1))
            ok = (dist >= 0) & (dist < WINDOW) if in_window else dist >= 0
            s = jnp.where(ok, s - slope * dist.astype(F32), -jnp.inf)
            m_new = jnp.maximum(m, jnp.max(s, axis=-1, keepdims=True))
            m_safe = jnp.where(m_new > -jnp.inf, m_new, 0.0)
            alpha = jnp.exp(m - m_safe)
            p = jnp.exp(s - m_safe)
            l = alpha * l + jnp.sum(p, axis=-1, keepdims=True)
            acc = alpha * acc + _dot(p.astype(BF16), vsw_ref[0, 0, pl.ds(k0, tk), :])
            return m_new, l, acc

        init = (jnp.full((r, 1), -jnp.inf, F32), jnp.zeros((r, 1), F32), jnp.zeros((r, LANES), F32))
        _, l, acc = lax.fori_loop(lo, hi, body, init)
        return acc / jnp.where(l > 0, l, 1.0)

    seq = ksa_ref.shape[2]
    tks, tkw = min(NSA_TKS, seq), min(NSA_TKW, seq)
    t_last = q0 + tq - 1
    o_s = attend(q_sel, ksa_ref, 0, t_last // tks + 1, tks, False)
    o_w = attend(q2, kwp_ref, jnp.maximum(q0 - (WINDOW - 1), 0) // tkw, t_last // tkw + 1, tkw, True)

    gt = gt_ref[0, 0]
    for i in range(NSA_REP):
        rs = slice(i * tq, (i + 1) * tq)
        o = (gt[:, 3 * i:3 * i + 1] * o_c[rs, 0:NSA_DK]
             + gt[:, 3 * i + 1:3 * i + 2] * o_s[rs, 0:NSA_DK]
             + gt[:, 3 * i + 2:3 * i + 3] * o_w[rs, NSA_DK:2 * NSA_DK])
        o_ref[0, :, i * NSA_DK:(i + 1) * NSA_DK] = o.astype(o_ref.dtype)


def _alibi_slopes(n):
    return np.asarray([2.0 ** (-8.0 * (i + 1) / n) for i in range(n)], dtype=np.float32)


def _cmp_to_slc_t(s, nc_pad):
    n_cmp = (s - CMP_BLOCK) // CMP_STRIDE + 1
    cs = np.arange(n_cmp) * CMP_STRIDE
    ss = np.arange(s // SLC_BLOCK) * SLC_BLOCK
    ov = np.clip(np.minimum(cs[:, None] + CMP_BLOCK, ss[None, :] + SLC_BLOCK)
                 - np.maximum(cs[:, None], ss[None, :]), 0, None)
    m = np.zeros((nc_pad, s // SLC_BLOCK), np.float32)
    m[:n_cmp] = ov / CMP_STRIDE
    return m.T


def _nsa_attn_call(q2, kc, vc, ksa, kwp, vsw, gt):
    nb, ng, nrep, s, _ = q2.shape
    tq = min(NSA_TQ, s)
    nc = kc.shape[2]
    nblk = s // SLC_BLOCK
    aw = LANES
    mt = jnp.asarray(_cmp_to_slc_t(s, nc), BF16)
    psel = np.zeros((nblk, aw), np.float32)
    psel[np.arange(nblk), NSA_DK + np.arange(nblk)] = 1.0
    seq = lambda w: pl.BlockSpec((1, 1, s, w), lambda b, g, i: (b, g, 0, 0))
    cmp_spec = pl.BlockSpec((1, 1, nc, LANES), lambda b, g, i: (b, g, 0, 0))
    return pl.pallas_call(
        _nsa_attn_kernel,
        grid=(nb, ng, s // tq),
        in_specs=[pl.BlockSpec(memory_space=pltpu.SMEM),
                  pl.BlockSpec((1, 1, nrep, tq, LANES), lambda b, g, i: (b, g, 0, i, 0)),
                  cmp_spec, cmp_spec, seq(aw), seq(LANES), seq(LANES),
                  pl.BlockSpec((1, 1, tq, LANES), lambda b, g, i: (b, g, i, 0)),
                  pl.BlockSpec((nblk, nc), lambda b, g, i: (0, 0)),
                  pl.BlockSpec((nblk, aw), lambda b, g, i: (0, 0))],
        out_specs=pl.BlockSpec((1, tq, nrep * NSA_DK), lambda b, g, i: (b, i, g)),
        out_shape=jax.ShapeDtypeStruct((nb, s, ng * nrep * NSA_DK), BF16),
        compiler_params=_cparams("parallel", "parallel", "arbitrary"),
        name="nsa_attention",
    )(jnp.asarray(_alibi_slopes(NSA_HEADS)), q2, kc, vc, ksa, kwp, vsw, gt, mt, jnp.asarray(psel, BF16))


def _out_proj_kernel(a0_ref, a1_ref, a2_ref, w0_ref, w1_ref, w2_ref, x_ref, g_ref, o_ref):
    acc = _dot(a0_ref[...], w0_ref[...]) + _dot(a1_ref[...], w1_ref[...]) + _dot(a2_ref[...], w2_ref[...])
    o_ref[...] = x_ref[...] + g_ref[0] * acc


def _out_proj_call(pool_o, gla_o, nsa_o, w_out, x2d, gate, seq, tm=1024, tn=1024):
    m, d = x2d.shape
    tm = min(tm, seq)
    k0, k1, k2 = pool_o.shape[1], gla_o.shape[1], nsa_o.shape[1]
    assert k0 == k1 and k2 == 2 * k0
    per_b = seq // tm
    a = lambda k: pl.BlockSpec((tm, k), lambda j, i: (i, 0))
    return pl.pallas_call(
        _out_proj_kernel,
        grid=(d // tn, m // tm),
        in_specs=[a(k0), a(k1), a(k2),
                  pl.BlockSpec((k0, tn), lambda j, i: (0, j)),
                  pl.BlockSpec((k1, tn), lambda j, i: (1, j)),
                  pl.BlockSpec((k2, tn), lambda j, i: (1, j)),
                  pl.BlockSpec((tm, tn), lambda j, i: (i, j)),
                  pl.BlockSpec((1, 1, tn), lambda j, i: (i // per_b, 0, j))],
        out_specs=pl.BlockSpec((tm, tn), lambda j, i: (i, j)),
        out_shape=jax.ShapeDtypeStruct((m, d), F32),
        compiler_params=_cparams("parallel", "parallel"),
        name="out_proj_residual",
    )(pool_o, gla_o, nsa_o, w_out, w_out, w_out, x2d, gate.reshape(-1, 1, d))


def _ffn_kernel(h_ref, wg_ref, wu_ref, wo_ref, x_ref, g_ref, o_ref, acc_ref):
    j = pl.program_id(1)

    @pl.when(j == 0)
    def _():
        acc_ref[...] = jnp.zeros_like(acc_ref)

    h = h_ref[...]
    act = (_silu(_dot(h, wg_ref[...])) * _dot(h, wu_ref[...])).astype(BF16)
    acc_ref[...] += _dot(act, wo_ref[...])

    @pl.when(j == pl.num_programs(1) - 1)
    def _():
        o_ref[...] = x_ref[...] + g_ref[0] * acc_ref[...]


def _ffn_call(h2d, w_in, w_out, x2d, gate, seq, tm=512, th=512):
    m, d = x2d.shape
    tm = min(tm, seq)
    hid = w_out.shape[0]
    nh = hid // th
    per_b = seq // tm
    return pl.pallas_call(
        _ffn_kernel,
        grid=(m // tm, nh),
        in_specs=[pl.BlockSpec((tm, d), lambda i, j: (i, 0)),
                  pl.BlockSpec((d, th), lambda i, j: (0, j)),
                  pl.BlockSpec((d, th), lambda i, j: (0, j + nh)),
                  pl.BlockSpec((th, d), lambda i, j: (j, 0)),
                  pl.BlockSpec((tm, d), lambda i, j: (i, 0)),
                  pl.BlockSpec((1, 1, d), lambda i, j: (i // per_b, 0, 0))],
        out_specs=pl.BlockSpec((tm, d), lambda i, j: (i, 0)),
        out_shape=jax.ShapeDtypeStruct((m, d), F32),
        scratch_shapes=[pltpu.VMEM((tm, d), F32)],
        compiler_params=_cparams("parallel", "arbitrary"),
        name="ffn_swiglu_residual",
    )(h2d, w_in, w_in, w_out, x2d, gate.reshape(-1, 1, d))


def _regroup_w_in(w):
    d = w.shape[0]
    a = 4 * POOL_GROUP + 2 * GLA_HEADS * GLA_DK + 2 * GLA_HEADS * GLA_DV
    b = a + GLA_RANK
    c = b + NSA_HEADS * NSA_DK + 6 * NSA_GROUPS * NSA_DK
    ng = 3 * NSA_HEADS
    main = jnp.concatenate([w[:, :a], w[:, b:c]], axis=1).astype(BF16)
    small = jnp.concatenate([w[:, a:b], w[:, c:c + ng], jnp.zeros((d, LANES - GLA_RANK - ng), w.dtype)],
                            axis=1).astype(BF16)
    return main, small


def _layer(x, mod, p):
    nb, s, d = x.shape
    sh_a, sc_a, g_a, sh_f, sc_f, g_f = [mod[:, i * d:(i + 1) * d] for i in range(6)]

    hmix = _norm_mod_call(x, p["norm_mix_gain"], sh_a, sc_a).reshape(nb * s, d)
    w_main, w_small = _regroup_w_in(p["w_in"])
    proj = _matmul_call(hmix, w_main, BF16, min(1024, s), 1536, "in_proj").reshape(nb, s, -1)
    small = _matmul_call(hmix, w_small, F32, min(1024, s), LANES, "in_proj_small").reshape(nb, s, LANES)

    pool_o = _pool_call(proj, p["pool_w"], p["pool_scale"])
    gla_o = _gla_call(proj, small, p["gla_w_gk_up"], p["gla_b_gk"], p["gla_norm_gain"])

    q2, ksa, kwp, vsw, gt = _nsa_prep_call(proj, small, p["nsa_q_gain"], p["nsa_k_gain"])
    kv0 = 4 * POOL_GROUP + 2 * GLA_HEADS * (GLA_DK + GLA_DV) + NSA_HEADS * NSA_DK
    kvw = NSA_GROUPS * NSA_DK

    def half_blocks(cols):
        t = cols.reshape(nb, s // CMP_STRIDE, CMP_STRIDE, NSA_GROUPS, NSA_DK)
        return jnp.transpose(t, (0, 3, 1, 2, 4)).reshape(nb, NSA_GROUPS, s // CMP_STRIDE, CMP_STRIDE * NSA_DK)

    kc, vc = _nsa_compress_call(half_blocks(proj[:, :, kv0:kv0 + kvw]),
                                half_blocks(proj[:, :, kv0 + kvw:kv0 + 2 * kvw]),
                                p["nsa_cmp_pos"], p["nsa_cmp_w1"], p["nsa_cmp_w2"], p["nsa_k_gain"][0])
    nsa_o = _nsa_attn_call(q2, kc, vc, ksa, kwp, vsw, gt)

    x2d = x.reshape(nb * s, d)
    x1 = _out_proj_call(pool_o.reshape(nb * s, -1), gla_o.reshape(nb * s, -1), nsa_o.reshape(nb * s, -1),
                        p["w_out"].astype(BF16), x2d, g_a, s)

    hf = _norm_mod_call(x1.reshape(nb, s, d), p["norm_ffn_gain"], sh_f, sc_f).reshape(nb * s, d)
    x2 = _ffn_call(hf, p["w_ffn_in"].astype(BF16), p["w_ffn_out"].astype(BF16), x1, g_f, s)
    return x2.reshape(nb, s, d)


_PER_LAYER = ("norm_mix_gain", "norm_ffn_gain", "w_in", "w_out", "pool_w", "pool_scale", "gla_w_gk_up",
              "gla_b_gk", "gla_norm_gain", "nsa_q_gain", "nsa_k_gain", "nsa_cmp_pos", "nsa_cmp_w1",
              "nsa_cmp_w2", "w_ffn_in", "w_ffn_out")


def kernel(x, c, norm_mix_gain, norm_ffn_gain, w_mod, b_mod, w_in, w_out, pool_w, pool_scale, gla_w_gk_up,
           gla_b_gk, gla_norm_gain, nsa_q_gain, nsa_k_gain, nsa_cmp_pos, nsa_cmp_w1, nsa_cmp_w2, w_ffn_in,
           w_ffn_out):
    params = dict(norm_mix_gain=norm_mix_gain, norm_ffn_gain=norm_ffn_gain, w_in=w_in, w_out=w_out,
                  pool_w=pool_w, pool_scale=pool_scale, gla_w_gk_up=gla_w_gk_up, gla_b_gk=gla_b_gk,
                  gla_norm_gain=gla_norm_gain, nsa_q_gain=nsa_q_gain, nsa_k_gain=nsa_k_gain,
                  nsa_cmp_pos=nsa_cmp_pos, nsa_cmp_w1=nsa_cmp_w1, nsa_cmp_w2=nsa_cmp_w2,
                  w_ffn_in=w_ffn_in, w_ffn_out=w_ffn_out)
    mod = _mod_call(c, w_mod, b_mod)
    for l in range(w_mod.shape[0]):
        x = _layer(x, mod[l], {k: params[k][l] for k in _PER_LAYER})
    return x
```

```python
import functools
import math

import jax
import jax.numpy as jnp
import numpy as np
from jax import lax
from jax.experimental import pallas as pl
from jax.experimental.pallas import tpu as pltpu

F32 = jnp.float32
BF16 = jnp.bfloat16

POOL_WINDOWS = (2, 4, 8, 16)
POOL_GROUP = 128
GLA_HEADS = 4
GLA_DK = 64
GLA_DV = 128
GLA_RANK = 16
GLA_GATE_NORM = 16.0
NSA_HEADS = 16
NSA_GROUPS = 4
NSA_REP = NSA_HEADS // NSA_GROUPS
NSA_DK = 64
CMP_BLOCK = 32
CMP_STRIDE = 16
SLC_BLOCK = 64
N_SELECT = 16
WINDOW = 512
FORCE_BONUS = 100.0
EPS = 1e-6

LANES = 128
VMEM_LIMIT = 56 * 1024 * 1024

GLA_SUB = 16
GLA_ROWS = 256
NSA_TQ = 128
NSA_TKS = 512
NSA_TKW = 128
MASK_BIG = 2.0 ** 100
LOG2E = math.log2(math.e)

_NT = (((1,), (1,)), ((), ()))
_TN = (((0,), (0,)), ((), ()))


def _cparams(*sem):
    return pltpu.CompilerParams(dimension_semantics=sem, vmem_limit_bytes=VMEM_LIMIT)


def _dot(a, b):
    return jnp.dot(a, b, preferred_element_type=F32)


def _split3(x):
    hi = x.astype(BF16)
    r1 = x - hi.astype(F32)
    mid = r1.astype(BF16)
    lo = (r1 - mid.astype(F32)).astype(BF16)
    return hi, mid, lo


def _silu(x):
    return x * jax.nn.sigmoid(x)


def _mod_kernel(ct_ref, w_ref, b_ref, o_ref):
    ct = ct_ref[...]
    act = _silu(ct)
    w = w_ref[0]
    for b in range(ct.shape[1]):
        row = jnp.sum(act[:, b:b + 1] * w, axis=0, keepdims=True)
        o_ref[0, b:b + 1, :] = row + b_ref[0]


def _mod_call(c, w_mod, b_mod, tn=512):
    nl, d, n = w_mod.shape
    nb = c.shape[0]
    return pl.pallas_call(
        _mod_kernel,
        grid=(nl, n // tn),
        in_specs=[pl.BlockSpec((d, nb), lambda l, j: (0, 0)),
                  pl.BlockSpec((1, d, tn), lambda l, j: (l, 0, j)),
                  pl.BlockSpec((1, 1, tn), lambda l, j: (l, 0, j))],
        out_specs=pl.BlockSpec((1, nb, tn), lambda l, j: (l, 0, j)),
        out_shape=jax.ShapeDtypeStruct((nl, nb, n), F32),
        compiler_params=_cparams("parallel", "parallel"),
        name="adaln_mod",
    )(c.T, w_mod, b_mod.reshape(nl, 1, n))


def _norm_mod_kernel(x_ref, g_ref, sh_ref, sc_ref, o_ref):
    x = x_ref[0]
    y = x * lax.rsqrt(jnp.mean(x * x, axis=-1, keepdims=True) + EPS) * g_ref[...]
    o_ref[0] = (y * (1.0 + sc_ref[0]) + sh_ref[0]).astype(o_ref.dtype)


def _norm_mod_call(x, gain, shift, scale, tm=512):
    nb, s, d = x.shape
    row = pl.BlockSpec((1, 1, d), lambda b, i: (b, 0, 0))
    return pl.pallas_call(
        _norm_mod_kernel,
        grid=(nb, s // tm),
        in_specs=[pl.BlockSpec((1, tm, d), lambda b, i: (b, i, 0)),
                  pl.BlockSpec((1, d), lambda b, i: (0, 0)),
                  row, row],
        out_specs=pl.BlockSpec((1, tm, d), lambda b, i: (b, i, 0)),
        out_shape=jax.ShapeDtypeStruct((nb, s, d), BF16),
        compiler_params=_cparams("parallel", "parallel"),
        name="norm_modulate",
    )(x, gain.reshape(1, d), shift.reshape(nb, 1, d), scale.reshape(nb, 1, d))


def _matmul_kernel(a_ref, w_ref, o_ref):
    o_ref[...] = _dot(a_ref[...], w_ref[...]).astype(o_ref.dtype)


def _matmul_call(a, w, out_dtype, tm, tn, name):
    m, k = a.shape
    n = w.shape[1]
    return pl.pallas_call(
        _matmul_kernel,
        grid=(n // tn, m // tm),
        in_specs=[pl.BlockSpec((tm, k), lambda j, i: (i, 0)),
                  pl.BlockSpec((k, tn), lambda j, i: (0, j))],
        out_specs=pl.BlockSpec((tm, tn), lambda j, i: (i, j)),
        out_shape=jax.ShapeDtypeStruct((m, n), out_dtype),
        compiler_params=_cparams("parallel", "parallel"),
        name=name,
    )(a, w)


def _pool_kernel(u_ref, w_ref, sc_ref, o_ref):
    s = u_ref.shape[1]
    t = lax.broadcasted_iota(jnp.int32, (s, 1), 0)
    for gi, win in enumerate(POOL_WINDOWS):
        cols = slice(gi * POOL_GROUP, (gi + 1) * POOL_GROUP)
        u = u_ref[0, :, cols].astype(F32)
        acc = u
        k = 1
        while k < win:
            acc = acc + jnp.where(t >= k, pltpu.roll(acc, k, axis=0), 0.0)
            k *= 2
        cnt = jnp.minimum(t + 1, win).astype(F32)
        dlt = (acc / cnt - u).astype(BF16)
        y = _dot(dlt, w_ref[gi])
        o_ref[0, :, cols] = (y * sc_ref[:, cols]).astype(o_ref.dtype)


def _pool_call(proj, pool_w, pool_scale):
    nb, s, _ = proj.shape
    pw = len(POOL_WINDOWS) * POOL_GROUP
    return pl.pallas_call(
        _pool_kernel,
        grid=(nb,),
        in_specs=[pl.BlockSpec((1, s, pw), lambda b: (b, 0, 0)),
                  pl.BlockSpec(pool_w.shape, lambda b: (0, 0, 0)),
                  pl.BlockSpec((1, pw), lambda b: (0, 0))],
        out_specs=pl.BlockSpec((1, s, pw), lambda b: (b, 0, 0)),
        out_shape=jax.ShapeDtypeStruct((nb, s, pw), BF16),
        compiler_params=_cparams("parallel"),
        name="pool_mixer",
    )(proj, pool_w.astype(BF16), pool_scale.reshape(1, pw))


def _gla_kernel(q_ref, k_ref, v_ref, g_ref, ps_ref, wup_ref, bgk_ref, gain_ref,
                tri_ref, ones_ref, seg_ref, o_ref,
                st_ref, qin_s, kdec_s, b_s, dl_s, qf_s, kf_s, o_s):
    rows = q_ref.shape[1]
    hk = GLA_HEADS * GLA_DK

    @pl.when(pl.program_id(1) == 0)
    def _():
        st_ref[...] = jnp.zeros_like(st_ref)

    low = ps_ref[0][:, :GLA_RANK]
    wup = wup_ref[...]
    x = jnp.broadcast_to(bgk_ref[...], (rows, hk))
    for r in range(GLA_RANK):
        x = x + low[:, r:r + 1] * wup[r:r + 1, :]
    la = (jnp.minimum(x, 0.0) - jnp.log1p(jnp.exp(-jnp.abs(x)))) * (1.0 / GLA_GATE_NORM)
    pieces = _split3(la)
    tri = tri_ref[...]
    ones = ones_ref[...]
    bcum = _dot(tri, pieces[0]) + _dot(tri, pieces[1]) + _dot(tri, pieces[2])
    tot = _dot(ones, pieces[0]) + _dot(ones, pieces[1]) + _dot(ones, pieces[2])
    q = q_ref[0].astype(F32) * (GLA_DK ** -0.5)
    k = k_ref[0].astype(F32)
    qf_s[...] = q
    kf_s[...] = k
    b_s[...] = bcum
    qin_s[...] = (q * jnp.exp(bcum)).astype(BF16)
    kdec_s[...] = (k * jnp.exp(tot - bcum)).astype(BF16)
    dl_s[...] = jnp.exp(tot)

    sub_i = lax.broadcasted_iota(jnp.int32, (GLA_SUB, 1), 0)
    seg = seg_ref[...]

    def step(n, carry):
        r0 = pl.multiple_of(n * GLA_SUB, GLA_SUB)
        rs = pl.ds(r0, GLA_SUB)
        bn = b_s[rs, :]
        qn = qf_s[rs, :]
        kn = kf_s[rs, :]
        vn = v_ref[0, rs, :]
        vf = vn.astype(F32)
        ws = []
        for j in range(GLA_SUB):
            e = jnp.exp(jnp.where(sub_i >= j, bn - bn[j:j + 1, :], -jnp.inf))
            ws.append((qn * (kn[j:j + 1, :] * e)).astype(BF16))
        z = _dot(jnp.concatenate(ws, axis=0), seg)
        intra = z[0:GLA_SUB, :] * vf[0:1, :]
        for j in range(1, GLA_SUB):
            intra = intra + z[j * GLA_SUB:(j + 1) * GLA_SUB, :] * vf[j:j + 1, :]
        qin = qin_s[rs, :]
        kdec = kdec_s[rs, :]
        dl = dl_s[pl.ds(r0, 1), :]
        st = st_ref[...]
        stb = st.astype(BF16)
        st_new = st * dl
        for h in range(GLA_HEADS):
            ck = slice(h * GLA_DK, (h + 1) * GLA_DK)
            cv = slice(h * GLA_DV, (h + 1) * GLA_DV)
            inter = lax.dot_general(qin[:, ck], stb[:, ck], _NT, preferred_element_type=F32)
            o_s[rs, cv] = inter + intra[:, cv]
            upd = lax.dot_general(vn[:, cv], kdec[:, ck], _TN, preferred_element_type=F32)
            st_ref[:, ck] = st_new[:, ck] + upd
        return carry

    lax.fori_loop(0, rows // GLA_SUB, step, 0)

    gain = gain_ref[...]
    for h in range(GLA_HEADS):
        cv = slice(h * GLA_DV, (h + 1) * GLA_DV)
        o = o_s[:, cv]
        y = o * lax.rsqrt(jnp.mean(o * o, axis=-1, keepdims=True) + EPS) * gain
        o_ref[0, :, cv] = (y * _silu(g_ref[0, :, cv].astype(F32))).astype(o_ref.dtype)


def _gla_call(proj, small, w_up, b_gk, gain):
    nb, s, _ = proj.shape
    rows = min(GLA_ROWS, s)
    hk, hv = GLA_HEADS * GLA_DK, GLA_HEADS * GLA_DV
    idx = np.arange(rows)
    same = (idx[:, None] // GLA_SUB) == (idx[None, :] // GLA_SUB)
    tri = jnp.asarray(same & (idx[None, :] <= idx[:, None]), BF16)
    ones = jnp.asarray(same, BF16)
    seg = jnp.asarray((np.arange(hk)[:, None] // GLA_DK) == (np.arange(hv)[None, :] // GLA_DV), BF16)
    const = lambda shape: pl.BlockSpec(shape, lambda b, i: (0,) * len(shape))
    return pl.pallas_call(
        _gla_kernel,
        grid=(nb, s // rows),
        in_specs=[pl.BlockSpec((1, rows, hk), lambda b, i: (b, i, 2)),
                  pl.BlockSpec((1, rows, hk), lambda b, i: (b, i, 3)),
                  pl.BlockSpec((1, rows, hv), lambda b, i: (b, i, 2)),
                  pl.BlockSpec((1, rows, hv), lambda b, i: (b, i, 3)),
                  pl.BlockSpec((1, rows, LANES), lambda b, i: (b, i, 0)),
                  const((GLA_RANK, hk)), const((1, hk)), const((1, GLA_DV)),
                  const((rows, rows)), const((rows, rows)), const((hk, hv))],
        out_specs=pl.BlockSpec((1, rows, hv), lambda b, i: (b, i, 0)),
        out_shape=jax.ShapeDtypeStruct((nb, s, hv), BF16),
        scratch_shapes=[pltpu.VMEM((GLA_DV, hk), F32),
                        pltpu.VMEM((rows, hk), BF16), pltpu.VMEM((rows, hk), BF16),
                        pltpu.VMEM((rows, hk), F32), pltpu.VMEM((rows, hk), F32),
                        pltpu.VMEM((rows, hk), F32), pltpu.VMEM((rows, hk), F32),
                        pltpu.VMEM((rows, hv), F32)],
        compiler_params=_cparams("parallel", "arbitrary"),
        name="gla",
    )(proj, proj, proj, proj, small, w_up, b_gk.reshape(1, hk), gain.reshape(1, GLA_DV), tri, ones, seg)


def _head_norm(x, gain):
    return x * lax.rsqrt(jnp.mean(x * x, axis=-1, keepdims=True) + EPS) * gain


def _nsa_prep_kernel(q_ref, ks_ref, vs_ref, kw_ref, vw_ref, ps_ref, qg_ref, kg_ref,
                     q2_ref, ksa_ref, kwp_ref, vst_ref, vwt_ref, gt_ref):
    ts = q_ref.shape[1]
    nblk = LANES - NSA_DK
    zeros = jnp.zeros((ts, NSA_DK), BF16)
    qg = qg_ref[...]
    kg = kg_ref[...]
    for h in range(NSA_HEADS):
        c = slice(h * NSA_DK, (h + 1) * NSA_DK)
        qn = _head_norm(q_ref[0, :, c].astype(F32), qg) * (NSA_DK ** -0.5 * LOG2E)
        q2_ref[0, h // NSA_REP, h % NSA_REP, :, 0:NSA_DK] = qn.astype(BF16)
        q2_ref[0, h // NSA_REP, h % NSA_REP, :, NSA_DK:] = jnp.zeros((ts, q2_ref.shape[4] - NSA_DK), BF16)
    pos = pl.program_id(1) * ts + lax.broadcasted_iota(jnp.int32, (ts, nblk), 0)
    blk = lax.broadcasted_iota(jnp.int32, (ts, nblk), 1)
    onehot = (jnp.right_shift(pos, int(math.log2(SLC_BLOCK))) == blk).astype(BF16)
    gates_t = jax.nn.sigmoid(ps_ref[0]).T
    vs_t = vs_ref[0].astype(F32).T
    vw_t = vw_ref[0].astype(F32).T
    gw = 3 * NSA_REP
    for g in range(NSA_GROUPS):
        c = slice(g * NSA_DK, (g + 1) * NSA_DK)
        ksa_ref[0, g, :, 0:NSA_DK] = _head_norm(ks_ref[0, :, c].astype(F32), kg[1:2, :]).astype(BF16)
        ksa_ref[0, g, :, NSA_DK:] = onehot
        kwp_ref[0, g, :, 0:NSA_DK] = _head_norm(kw_ref[0, :, c].astype(F32), kg[2:3, :]).astype(BF16)
        kwp_ref[0, g, :, NSA_DK:] = zeros
        vst_ref[0, g] = vs_t[c, :].astype(BF16)
        vwt_ref[0, g] = vw_t[c, :].astype(BF16)
        gt_ref[0, g, 0:gw, :] = gates_t[GLA_RANK + g * gw:GLA_RANK + (g + 1) * gw, :]
        gt_ref[0, g, gw:, :] = jnp.zeros((gt_ref.shape[2] - gw, ts), F32)


def _nsa_prep_call(proj, small, q_gain, k_gain, ts=512):
    nb, s, _ = proj.shape
    ts = min(ts, s)
    assert s // SLC_BLOCK <= LANES - NSA_DK
    kvw = NSA_GROUPS * NSA_DK
    qw = NSA_HEADS * NSA_DK
    col = lambda j: (lambda b, i: (b, i, j))
    grp = lambda w: pl.BlockSpec((1, NSA_GROUPS, ts, w), lambda b, i: (b, 0, i, 0))
    grp_t = lambda rows: pl.BlockSpec((1, NSA_GROUPS, rows, ts), lambda b, i: (b, 0, 0, i))
    gate_rows = 16
    return pl.pallas_call(
        _nsa_prep_kernel,
        grid=(nb, s // ts),
        in_specs=[pl.BlockSpec((1, ts, qw), col(2)),
                  pl.BlockSpec((1, ts, kvw), col(14)),
                  pl.BlockSpec((1, ts, kvw), col(15)),
                  pl.BlockSpec((1, ts, kvw), col(16)),
                  pl.BlockSpec((1, ts, kvw), col(17)),
                  pl.BlockSpec((1, ts, LANES), col(0)),
                  pl.BlockSpec((1, NSA_DK), lambda b, i: (0, 0)),
                  pl.BlockSpec((3, NSA_DK), lambda b, i: (0, 0))],
        out_specs=[pl.BlockSpec((1, NSA_GROUPS, NSA_REP, ts, LANES), lambda b, i: (b, 0, 0, i, 0)),
                   grp(LANES), grp(LANES), grp_t(NSA_DK), grp_t(NSA_DK), grp_t(gate_rows)],
        out_shape=[jax.ShapeDtypeStruct((nb, NSA_GROUPS, NSA_REP, s, LANES), BF16),
                   jax.ShapeDtypeStruct((nb, NSA_GROUPS, s, LANES), BF16),
                   jax.ShapeDtypeStruct((nb, NSA_GROUPS, s, LANES), BF16),
                   jax.ShapeDtypeStruct((nb, NSA_GROUPS, NSA_DK, s), BF16),
                   jax.ShapeDtypeStruct((nb, NSA_GROUPS, NSA_DK, s), BF16),
                   jax.ShapeDtypeStruct((nb, NSA_GROUPS, gate_rows, s), F32)],
        compiler_params=_cparams("parallel", "parallel"),
        name="nsa_prep",
    )(proj, proj, proj, proj, proj, small, q_gain.reshape(1, NSA_DK), k_gain)


def _nsa_compress_kernel(hk_ref, hv_ref, pos_ref, w1_ref, w2_ref, kg_ref, kc_ref, vc_ref):
    nh = hk_ref.shape[2]
    half = CMP_STRIDE * NSA_DK
    row = lax.broadcasted_iota(jnp.int32, (nh, 1), 0)
    for kv, (h_ref, o_ref) in enumerate(((hk_ref, kc_ref), (hv_ref, vc_ref))):
        hb = h_ref[0, 0].astype(F32)
        top = _dot((hb + pos_ref[kv, 0:1, :]).astype(BF16), w1_ref[kv, 0:half, :])
        bot = _dot((hb + pos_ref[kv, 1:2, :]).astype(BF16), w1_ref[kv, half:, :])
        hid = _silu(top + pltpu.roll(bot, nh - 1, axis=0))
        out = _dot(hid.astype(BF16), w2_ref[kv])
        if kv == 0:
            out = _head_norm(out, kg_ref[...])
        out = jnp.where(row < nh - 1, out, 0.0)
        if kv == 0:
            o_ref[0, 0, :, 0:NSA_DK] = out.astype(BF16)
            o_ref[0, 0, :, NSA_DK:] = jnp.zeros((nh, LANES - NSA_DK), BF16)
        else:
            o_ref[0, 0] = out.T.astype(BF16)


def _nsa_compress_call(hk, hv, pos, w1, w2, kgain0):
    nb, ng, nh, hw = hk.shape
    hid = w1.shape[2]
    blk = pl.BlockSpec((1, 1, nh, hw), lambda b, g: (b, g, 0, 0))
    out = pl.BlockSpec((1, 1, nh, LANES), lambda b, g: (b, g, 0, 0))
    return pl.pallas_call(
        _nsa_compress_kernel,
        grid=(nb, ng),
        in_specs=[blk, blk,
                  pl.BlockSpec((2, 2, hw), lambda b, g: (0, 0, 0)),
                  pl.BlockSpec((2, 2 * hw, hid), lambda b, g: (0, 0, 0)),
                  pl.BlockSpec((2, hid, NSA_DK), lambda b, g: (0, 0, 0)),
                  pl.BlockSpec((1, NSA_DK), lambda b, g: (0, 0))],
        out_specs=[out, pl.BlockSpec((1, 1, NSA_DK, nh), lambda b, g: (b, g, 0, 0))],
        out_shape=[jax.ShapeDtypeStruct((nb, ng, nh, LANES), BF16),
                   jax.ShapeDtypeStruct((nb, ng, NSA_DK, nh), BF16)],
        compiler_params=_cparams("parallel", "parallel"),
        name="nsa_compress",
    )(hk, hv, pos.reshape(2, 2, hw), w1.astype(BF16), w2.astype(BF16), kgain0.reshape(1, NSA_DK))


def _softmax_cols(s):
    m = jnp.max(s, axis=0, keepdims=True)
    m = jnp.where(m > -jnp.inf, m, 0.0)
    e = jnp.exp2(s - m)
    d = jnp.sum(e, axis=0, keepdims=True)
    return e, 1.0 / jnp.where(d > 0, d, 1.0)


def _nsa_attn_kernel(slopes_ref, q_ref, kc_ref, vct_ref, ksa_ref, kwp_ref, vst_ref, vwt_ref, gt_ref,
                     mt_ref, psel_ref, wb_ref, o_ref, sa0_ref, sa1_ref, sp0_ref, sp1_ref):
    tq = q_ref.shape[3]
    r = NSA_REP * tq
    nc = kc_ref.shape[2]
    nblk = mt_ref.shape[0]
    seq = ksa_ref.shape[2]
    g = pl.program_id(1)
    q0 = pl.program_id(2) * tq

    q2 = q_ref[0, 0].reshape(r, LANES)
    lane = lax.broadcasted_iota(jnp.int32, (1, r), 1)
    rep = jnp.zeros((1, r), jnp.int32)
    for i in range(1, NSA_REP):
        rep = rep + (lane >= i * tq).astype(jnp.int32)
    t_q = q0 + (lane - rep * tq)
    slope = jnp.zeros((1, r), F32)
    for i in range(NSA_REP):
        slope = jnp.where(rep == i, slopes_ref[g * NSA_REP + i], slope)

    s_c = lax.dot_general(kc_ref[0, 0], q2, _NT, preferred_element_type=F32)
    cend = lax.broadcasted_iota(jnp.int32, (nc, r), 0) * CMP_STRIDE + (CMP_BLOCK - 1)
    dist_c = t_q - cend
    e_c, inv_c = _softmax_cols(jnp.where(dist_c >= 0, s_c - slope * dist_c.astype(F32), -jnp.inf))
    o_c = _dot(vct_ref[0, 0], e_c.astype(BF16)) * inv_c

    p_sum = e_c[:, 0:tq] * inv_c[:, 0:tq]
    for i in range(1, NSA_REP):
        p_sum = p_sum + e_c[:, i * tq:(i + 1) * tq] * inv_c[:, i * tq:(i + 1) * tq]
    mt = mt_ref[...]
    imp = sum(_dot(mt, piece) for piece in _split3(p_sum))
    blk = lax.broadcasted_iota(jnp.int32, (nblk, 1), 0).astype(F32)
    t_row = q0 + lax.broadcasted_iota(jnp.int32, (1, tq), 1)
    cur = jnp.right_shift(t_row, int(math.log2(SLC_BLOCK))).astype(F32)
    forced = (blk == 0.0) | (blk == cur) | (blk == cur - 1.0)
    valid = blk <= cur
    score = jnp.where(valid, imp + FORCE_BONUS * forced.astype(F32), -jnp.inf)
    sel = jnp.zeros((nblk, tq), F32)
    for _ in range(min(N_SELECT, nblk)):
        best = jnp.max(score, axis=0, keepdims=True)
        first = jnp.min(jnp.where(score == best, blk, float(nblk)), axis=0, keepdims=True)
        pick = blk == first
        sel = jnp.where(pick, 1.0, sel)
        score = jnp.where(pick, -jnp.inf, score)
    sel_bias = jnp.where((sel > 0.0) & valid, 0.0, -MASK_BIG).astype(BF16)
    bias_q = lax.dot_general(sel_bias, psel_ref[...], _TN, preferred_element_type=F32)
    bias_q = bias_q.astype(BF16)
    q_sel = jnp.concatenate([q_ref[0, 0, i] + bias_q for i in range(NSA_REP)], axis=0)

    tks = min(NSA_TKS, seq)
    pos0 = lax.broadcasted_iota(jnp.int32, (tks, r), 0)
    sb = slope * pos0.astype(F32)

    def sel_scores(j):
        k0 = pl.multiple_of(j * tks, tks)
        return lax.dot_general(ksa_ref[0, 0, pl.ds(k0, tks), :], q_sel, _NT, preferred_element_type=F32) + sb

    def sel_values(j, p):
        return _dot(vst_ref[0, 0, :, pl.ds(pl.multiple_of(j * tks, tks), tks)], p)

    def sel_softmax(j, a, m, l):
        c = slope * (j * tks + jnp.zeros((1, r), jnp.int32)).astype(F32)
        m_new = jnp.maximum(m, jnp.max(a, axis=0, keepdims=True) + c)
        m_safe = jnp.where(m_new > -jnp.inf, m_new, 0.0)
        p = jnp.exp2(a - (m_safe - c))
        alpha = jnp.exp2(m - m_safe)
        return p.astype(BF16), alpha, m_new, alpha * l + jnp.sum(p, axis=0, keepdims=True)

    def sel_step(j, a_cur, a_next, p_prev, p_cur, carry):
        m, l, acc = carry
        a_next[...] = sel_scores(j + 1)
        acc = acc + sel_values(jnp.maximum(j - 1, 0), p_prev[...])
        p, alpha, m, l = sel_softmax(j, a_cur[...], m, l)
        p_cur[...] = p
        return m, l, alpha * acc

    def sel_pair(i, carry):
        carry = sel_step(2 * i, sa0_ref, sa1_ref, sp1_ref, sp0_ref, carry)
        return sel_step(2 * i + 1, sa1_ref, sa0_ref, sp0_ref, sp1_ref, carry)

    def sel_odd(carry):
        carry = sel_step(n_big - 1, sa0_ref, sa1_ref, sp1_ref, sp0_ref, carry)
        sa0_ref[...] = sa1_ref[...]
        sp1_ref[...] = sp0_ref[...]
        return carry

    n_big = q0 // tks
    sa0_ref[...] = sel_scores(0)
    sp1_ref[...] = jnp.zeros((tks, r), BF16)
    carry = (jnp.full((1, r), -jnp.inf, F32), jnp.zeros((1, r), F32), jnp.zeros((NSA_DK, r), F32))
    carry = lax.fori_loop(0, n_big // 2, sel_pair, carry)
    m_s, l_s, acc_s = lax.cond(n_big % 2 == 1, sel_odd, lambda cr: cr, carry)
    acc_s = acc_s + sel_values(jnp.maximum(n_big - 1, 0), sp1_ref[...])
    a_d = jnp.where(n_big * tks + pos0 <= t_q, sa0_ref[...], -jnp.inf)
    p_d, alpha_d, _, l_s = sel_softmax(n_big, a_d, m_s, l_s)
    acc_s = alpha_d * acc_s + sel_values(n_big, p_d)
    o_s = acc_s * (1.0 / jnp.where(l_s > 0, l_s, 1.0))

    span = WINDOW + tq

    def window(start, bias_fn):
        s_w = lax.dot_general(kwp_ref[0, 0, pl.ds(start, span), :], q2, _NT, preferred_element_type=F32)
        e_w, inv_w = _softmax_cols(bias_fn(s_w))
        return _dot(vwt_ref[0, 0, :, pl.ds(start, span)], e_w.astype(BF16)) * inv_w

    def window_head(s_w):
        dist_w = t_q - lax.broadcasted_iota(jnp.int32, (span, r), 0)
        return jnp.where((dist_w >= 0) & (dist_w < WINDOW), s_w - slope * dist_w.astype(F32), -jnp.inf)

    o_w = lax.cond(q0 >= WINDOW,
                   lambda: window(pl.multiple_of(q0 - WINDOW, tq), lambda s_w: s_w + wb_ref[0]),
                   lambda: window(0, window_head))

    gt = gt_ref[0, 0]
    outs = []
    for i in range(NSA_REP):
        ls = slice(i * tq, (i + 1) * tq)
        outs.append(gt[3 * i:3 * i + 1, :] * o_c[:, ls] + gt[3 * i + 1:3 * i + 2, :] * o_s[:, ls]
                    + gt[3 * i + 2:3 * i + 3, :] * o_w[:, ls])
    o_ref[0] = jnp.concatenate(outs, axis=0).T.astype(o_ref.dtype)


def _alibi_slopes(n):
    return np.asarray([2.0 ** (-8.0 * (i + 1) / n) for i in range(n)], dtype=np.float32)


def _cmp_to_slc_t(s, nc_pad):
    n_cmp = (s - CMP_BLOCK) // CMP_STRIDE + 1
    cs = np.arange(n_cmp) * CMP_STRIDE
    ss = np.arange(s // SLC_BLOCK) * SLC_BLOCK
    ov = np.clip(np.minimum(cs[:, None] + CMP_BLOCK, ss[None, :] + SLC_BLOCK)
                 - np.maximum(cs[:, None], ss[None, :]), 0, None)
    m = np.zeros((nc_pad, s // SLC_BLOCK), np.float32)
    m[:n_cmp] = ov / CMP_STRIDE
    return m.T


def _nsa_attn_call(q2, kc, vct, ksa, kwp, vst, vwt, gt):
    nb, ng, nrep, s, _ = q2.shape
    tq = min(NSA_TQ, s)
    tks = min(NSA_TKS, s)
    assert s >= WINDOW + tq and WINDOW % tq == 0
    nc = kc.shape[2]
    nblk = s // SLC_BLOCK
    mt = jnp.asarray(_cmp_to_slc_t(s, nc), BF16)
    psel = np.zeros((nblk, LANES), np.float32)
    psel[np.arange(nblk), NSA_DK + np.arange(nblk)] = 1.0
    slopes = _alibi_slopes(NSA_HEADS).astype(np.float64) * LOG2E
    dist = WINDOW + np.arange(tq)[None, :] - np.arange(WINDOW + tq)[:, None]
    wbias = np.where((dist >= 0) & (dist < WINDOW), -slopes[:, None, None] * dist[None], -np.inf)
    wbias = wbias.reshape(ng, nrep, WINDOW + tq, tq).transpose(0, 2, 1, 3).reshape(ng, WINDOW + tq, nrep * tq)
    seq = pl.BlockSpec((1, 1, s, LANES), lambda b, g, i: (b, g, 0, 0))
    seq_t = pl.BlockSpec((1, 1, NSA_DK, s), lambda b, g, i: (b, g, 0, 0))
    return pl.pallas_call(
        _nsa_attn_kernel,
        grid=(nb, ng, s // tq),
        in_specs=[pl.BlockSpec(memory_space=pltpu.SMEM),
                  pl.BlockSpec((1, 1, nrep, tq, LANES), lambda b, g, i: (b, g, 0, i, 0)),
                  pl.BlockSpec((1, 1, nc, LANES), lambda b, g, i: (b, g, 0, 0)),
                  pl.BlockSpec((1, 1, NSA_DK, nc), lambda b, g, i: (b, g, 0, 0)),
                  seq, seq, seq_t, seq_t,
                  pl.BlockSpec((1, 1, gt.shape[2], tq), lambda b, g, i: (b, g, 0, i)),
                  pl.BlockSpec((nblk, nc), lambda b, g, i: (0, 0)),
                  pl.BlockSpec((nblk, LANES), lambda b, g, i: (0, 0)),
                  pl.BlockSpec((1, WINDOW + tq, nrep * tq), lambda b, g, i: (g, 0, 0))],
        out_specs=pl.BlockSpec((1, tq, nrep * NSA_DK), lambda b, g, i: (b, i, g)),
        out_shape=jax.ShapeDtypeStruct((nb, s, ng * nrep * NSA_DK), BF16),
        scratch_shapes=[pltpu.VMEM((tks, nrep * tq), F32), pltpu.VMEM((tks, nrep * tq), F32),
                        pltpu.VMEM((tks, nrep * tq), BF16), pltpu.VMEM((tks, nrep * tq), BF16)],
        compiler_params=_cparams("parallel", "parallel", "arbitrary"),
        name="nsa_attention",
    )(jnp.asarray(slopes, F32), q2, kc, vct, ksa, kwp, vst, vwt, gt, mt, jnp.asarray(psel, BF16),
      jnp.asarray(wbias, F32))


def _out_proj_kernel(a0_ref, a1_ref, a2_ref, w0_ref, w1_ref, w2_ref, x_ref, g_ref, o_ref):
    acc = _dot(a0_ref[...], w0_ref[...]) + _dot(a1_ref[...], w1_ref[...]) + _dot(a2_ref[...], w2_ref[...])
    o_ref[...] = x_ref[...] + g_ref[0] * acc


def _out_proj_call(pool_o, gla_o, nsa_o, w_out, x2d, gate, seq, tm=1024, tn=1024):
    m, d = x2d.shape
    tm = min(tm, seq)
    k0, k1, k2 = pool_o.shape[1], gla_o.shape[1], nsa_o.shape[1]
    assert k0 == k1 and k2 == 2 * k0
    per_b = seq // tm
    a = lambda k: pl.BlockSpec((tm, k), lambda j, i: (i, 0))
    return pl.pallas_call(
        _out_proj_kernel,
        grid=(d // tn, m // tm),
        in_specs=[a(k0), a(k1), a(k2),
                  pl.BlockSpec((k0, tn), lambda j, i: (0, j)),
                  pl.BlockSpec((k1, tn), lambda j, i: (1, j)),
                  pl.BlockSpec((k2, tn), lambda j, i: (1, j)),
                  pl.BlockSpec((tm, tn), lambda j, i: (i, j)),
                  pl.BlockSpec((1, 1, tn), lambda j, i: (i // per_b, 0, j))],
        out_specs=pl.BlockSpec((tm, tn), lambda j, i: (i, j)),
        out_shape=jax.ShapeDtypeStruct((m, d), F32),
        compiler_params=_cparams("parallel", "parallel"),
        name="out_proj_residual",
    )(pool_o, gla_o, nsa_o, w_out, w_out, w_out, x2d, gate.reshape(-1, 1, d))


def _ffn_kernel(h_ref, wg_ref, wu_ref, wo_ref, x_ref, g_ref, o_ref, acc_ref):
    j = pl.program_id(1)

    @pl.when(j == 0)
    def _():
        acc_ref[...] = jnp.zeros_like(acc_ref)

    h = h_ref[...]
    act = (_silu(_dot(h, wg_ref[...])) * _dot(h, wu_ref[...])).astype(BF16)
    acc_ref[...] += _dot(act, wo_ref[...])

    @pl.when(j == pl.num_programs(1) - 1)
    def _():
        o_ref[...] = x_ref[...] + g_ref[0] * acc_ref[...]


def _ffn_call(h2d, w_in, w_out, x2d, gate, seq, tm=512, th=512):
    m, d = x2d.shape
    tm = min(tm, seq)
    hid = w_out.shape[0]
    nh = hid // th
    per_b = seq // tm
    return pl.pallas_call(
        _ffn_kernel,
        grid=(m // tm, nh),
        in_specs=[pl.BlockSpec((tm, d), lambda i, j: (i, 0)),
                  pl.BlockSpec((d, th), lambda i, j: (0, j)),
                  pl.BlockSpec((d, th), lambda i, j: (0, j + nh)),
                  pl.BlockSpec((th, d), lambda i, j: (j, 0)),
                  pl.BlockSpec((tm, d), lambda i, j: (i, 0)),
                  pl.BlockSpec((1, 1, d), lambda i, j: (i // per_b, 0, 0))],
        out_specs=pl.BlockSpec((tm, d), lambda i, j: (i, 0)),
        out_shape=jax.ShapeDtypeStruct((m, d), F32),
        scratch_shapes=[pltpu.VMEM((tm, d), F32)],
        compiler_params=_cparams("parallel", "arbitrary"),
        name="ffn_swiglu_residual",
    )(h2d, w_in, w_in, w_out, x2d, gate.reshape(-1, 1, d))


def _regroup_w_in(w):
    d = w.shape[0]
    a = 4 * POOL_GROUP + 2 * GLA_HEADS * GLA_DK + 2 * GLA_HEADS * GLA_DV
    b = a + GLA_RANK
    c = b + NSA_HEADS * NSA_DK + 6 * NSA_GROUPS * NSA_DK
    ng = 3 * NSA_HEADS
    main = jnp.concatenate([w[:, :a], w[:, b:c]], axis=1).astype(BF16)
    small = jnp.concatenate([w[:, a:b], w[:, c:c + ng], jnp.zeros((d, LANES - GLA_RANK - ng), w.dtype)],
                            axis=1).astype(BF16)
    return main, small


def _layer(x, mod, p):
    nb, s, d = x.shape
    sh_a, sc_a, g_a, sh_f, sc_f, g_f = [mod[:, i * d:(i + 1) * d] for i in range(6)]

    hmix = _norm_mod_call(x, p["norm_mix_gain"], sh_a, sc_a).reshape(nb * s, d)
    w_main, w_small = _regroup_w_in(p["w_in"])
    proj = _matmul_call(hmix, w_main, BF16, min(1024, s), 1536, "in_proj").reshape(nb, s, -1)
    small = _matmul_call(hmix, w_small, F32, min(1024, s), LANES, "in_proj_small").reshape(nb, s, LANES)

    pool_o = _pool_call(proj, p["pool_w"], p["pool_scale"])
    gla_o = _gla_call(proj, small, p["gla_w_gk_up"], p["gla_b_gk"], p["gla_norm_gain"])

    q2, ksa, kwp, vst, vwt, gt = _nsa_prep_call(proj, small, p["nsa_q_gain"], p["nsa_k_gain"])
    kv0 = 4 * POOL_GROUP + 2 * GLA_HEADS * (GLA_DK + GLA_DV) + NSA_HEADS * NSA_DK
    kvw = NSA_GROUPS * NSA_DK

    def half_blocks(cols):
        t = cols.reshape(nb, s // CMP_STRIDE, CMP_STRIDE, NSA_GROUPS, NSA_DK)
        return jnp.transpose(t, (0, 3, 1, 2, 4)).reshape(nb, NSA_GROUPS, s // CMP_STRIDE, CMP_STRIDE * NSA_DK)

    kc, vc = _nsa_compress_call(half_blocks(proj[:, :, kv0:kv0 + kvw]),
                                half_blocks(proj[:, :, kv0 + kvw:kv0 + 2 * kvw]),
                                p["nsa_cmp_pos"], p["nsa_cmp_w1"], p["nsa_cmp_w2"], p["nsa_k_gain"][0])
    nsa_o = _nsa_attn_call(q2, kc, vc, ksa, kwp, vst, vwt, gt)

    x2d = x.reshape(nb * s, d)
    x1 = _out_proj_call(pool_o.reshape(nb * s, -1), gla_o.reshape(nb * s, -1), nsa_o.reshape(nb * s, -1),
                        p["w_out"].astype(BF16), x2d, g_a, s)

    hf = _norm_mod_call(x1.reshape(nb, s, d), p["norm_ffn_gain"], sh_f, sc_f).reshape(nb * s, d)
    x2 = _ffn_call(hf, p["w_ffn_in"].astype(BF16), p["w_ffn_out"].astype(BF16), x1, g_f, s)
    return x2.reshape(nb, s, d)


_PER_LAYER = ("norm_mix_gain", "norm_ffn_gain", "w_in", "w_out", "pool_w", "pool_scale", "gla_w_gk_up",
              "gla_b_gk", "gla_norm_gain", "nsa_q_gain", "nsa_k_gain", "nsa_cmp_pos", "nsa_cmp_w1",
              "nsa_cmp_w2", "w_ffn_in", "w_ffn_out")


def kernel(x, c, norm_mix_gain, norm_ffn_gain, w_mod, b_mod, w_in, w_out, pool_w, pool_scale, gla_w_gk_up,
           gla_b_gk, gla_norm_gain, nsa_q_gain, nsa_k_gain, nsa_cmp_pos, nsa_cmp_w1, nsa_cmp_w2, w_ffn_in,
           w_ffn_out):
    params = dict(norm_mix_gain=norm_mix_gain, norm_ffn_gain=norm_ffn_gain, w_in=w_in, w_out=w_out,
                  pool_w=pool_w, pool_scale=pool_scale, gla_w_gk_up=gla_w_gk_up, gla_b_gk=gla_b_gk,
                  gla_norm_gain=gla_norm_gain, nsa_q_gain=nsa_q_gain, nsa_k_gain=nsa_k_gain,
                  nsa_cmp_pos=nsa_cmp_pos, nsa_cmp_w1=nsa_cmp_w1, nsa_cmp_w2=nsa_cmp_w2,
                  w_ffn_in=w_ffn_in, w_ffn_out=w_ffn_out)
    mod = _mod_call(c, w_mod, b_mod)
    for l in range(w_mod.shape[0]):
        x = _layer(x, mod[l], {k: params[k][l] for k in _PER_LAYER})
    return x
```

```python
import functools
import math

import jax
import jax.numpy as jnp
import numpy as np
from jax import lax
from jax.experimental import pallas as pl
from jax.experimental.pallas import tpu as pltpu

F32 = jnp.float32
BF16 = jnp.bfloat16

POOL_WINDOWS = (2, 4, 8, 16)
POOL_GROUP = 128
GLA_HEADS = 4
GLA_DK = 64
GLA_DV = 128
GLA_RANK = 16
GLA_GATE_NORM = 16.0
NSA_HEADS = 16
NSA_GROUPS = 4
NSA_REP = NSA_HEADS // NSA_GROUPS
NSA_DK = 64
CMP_BLOCK = 32
CMP_STRIDE = 16
SLC_BLOCK = 64
N_SELECT = 16
WINDOW = 512
FORCE_BONUS = 100.0
EPS = 1e-6

LANES = 128
VMEM_LIMIT = 56 * 1024 * 1024

GLA_SUB = 16
GLA_ROWS = 256
NSA_TSEL = 512
NSA_TQ = 256
NSA_TKS = 512
NSA_TKW = 128
MASK_BIG = 2.0 ** 100
LOG2E = math.log2(math.e)

_NT = (((1,), (1,)), ((), ()))
_TN = (((0,), (0,)), ((), ()))


def _cparams(*sem):
    return pltpu.CompilerParams(dimension_semantics=sem, vmem_limit_bytes=VMEM_LIMIT)


def _dot(a, b):
    return jnp.dot(a, b, preferred_element_type=F32)


def _split3(x):
    hi = x.astype(BF16)
    r1 = x - hi.astype(F32)
    mid = r1.astype(BF16)
    lo = (r1 - mid.astype(F32)).astype(BF16)
    return hi, mid, lo


def _silu(x):
    return x * jax.nn.sigmoid(x)


def _mod_kernel(ct_ref, w_ref, b_ref, o_ref):
    ct = ct_ref[...]
    act = _silu(ct)
    w = w_ref[0]
    for b in range(ct.shape[1]):
        row = jnp.sum(act[:, b:b + 1] * w, axis=0, keepdims=True)
        o_ref[0, b:b + 1, :] = row + b_ref[0]


def _mod_call(c, w_mod, b_mod, tn=512):
    nl, d, n = w_mod.shape
    nb = c.shape[0]
    return pl.pallas_call(
        _mod_kernel,
        grid=(nl, n // tn),
        in_specs=[pl.BlockSpec((d, nb), lambda l, j: (0, 0)),
                  pl.BlockSpec((1, d, tn), lambda l, j: (l, 0, j)),
                  pl.BlockSpec((1, 1, tn), lambda l, j: (l, 0, j))],
        out_specs=pl.BlockSpec((1, nb, tn), lambda l, j: (l, 0, j)),
        out_shape=jax.ShapeDtypeStruct((nl, nb, n), F32),
        compiler_params=_cparams("parallel", "parallel"),
        name="adaln_mod",
    )(c.T, w_mod, b_mod.reshape(nl, 1, n))


def _norm_mod_kernel(x_ref, g_ref, sh_ref, sc_ref, o_ref):
    x = x_ref[0]
    y = x * lax.rsqrt(jnp.mean(x * x, axis=-1, keepdims=True) + EPS) * g_ref[...]
    o_ref[0] = (y * (1.0 + sc_ref[0]) + sh_ref[0]).astype(o_ref.dtype)


def _norm_mod_call(x, gain, shift, scale, tm=512):
    nb, s, d = x.shape
    row = pl.BlockSpec((1, 1, d), lambda b, i: (b, 0, 0))
    return pl.pallas_call(
        _norm_mod_kernel,
        grid=(nb, s // tm),
        in_specs=[pl.BlockSpec((1, tm, d), lambda b, i: (b, i, 0)),
                  pl.BlockSpec((1, d), lambda b, i: (0, 0)),
                  row, row],
        out_specs=pl.BlockSpec((1, tm, d), lambda b, i: (b, i, 0)),
        out_shape=jax.ShapeDtypeStruct((nb, s, d), BF16),
        compiler_params=_cparams("parallel", "parallel"),
        name="norm_modulate",
    )(x, gain.reshape(1, d), shift.reshape(nb, 1, d), scale.reshape(nb, 1, d))


def _matmul_kernel(a_ref, w_ref, o_ref):
    o_ref[...] = _dot(a_ref[...], w_ref[...]).astype(o_ref.dtype)


def _matmul_call(a, w, out_dtype, tm, tn, name):
    m, k = a.shape
    n = w.shape[1]
    return pl.pallas_call(
        _matmul_kernel,
        grid=(n // tn, m // tm),
        in_specs=[pl.BlockSpec((tm, k), lambda j, i: (i, 0)),
                  pl.BlockSpec((k, tn), lambda j, i: (0, j))],
        out_specs=pl.BlockSpec((tm, tn), lambda j, i: (i, j)),
        out_shape=jax.ShapeDtypeStruct((m, n), out_dtype),
        compiler_params=_cparams("parallel", "parallel"),
        name=name,
    )(a, w)


def _pool_kernel(u_ref, w_ref, sc_ref, o_ref):
    s = u_ref.shape[1]
    t = lax.broadcasted_iota(jnp.int32, (s, 1), 0)
    for gi, win in enumerate(POOL_WINDOWS):
        cols = slice(gi * POOL_GROUP, (gi + 1) * POOL_GROUP)
        u = u_ref[0, :, cols].astype(F32)
        acc = u
        k = 1
        while k < win:
            acc = acc + jnp.where(t >= k, pltpu.roll(acc, k, axis=0), 0.0)
            k *= 2
        cnt = jnp.minimum(t + 1, win).astype(F32)
        dlt = (acc / cnt - u).astype(BF16)
        y = _dot(dlt, w_ref[gi])
        o_ref[0, :, cols] = (y * sc_ref[:, cols]).astype(o_ref.dtype)


def _pool_call(proj, pool_w, pool_scale):
    nb, s, _ = proj.shape
    pw = len(POOL_WINDOWS) * POOL_GROUP
    return pl.pallas_call(
        _pool_kernel,
        grid=(nb,),
        in_specs=[pl.BlockSpec((1, s, pw), lambda b: (b, 0, 0)),
                  pl.BlockSpec(pool_w.shape, lambda b: (0, 0, 0)),
                  pl.BlockSpec((1, pw), lambda b: (0, 0))],
        out_specs=pl.BlockSpec((1, s, pw), lambda b: (b, 0, 0)),
        out_shape=jax.ShapeDtypeStruct((nb, s, pw), BF16),
        compiler_params=_cparams("parallel"),
        name="pool_mixer",
    )(proj, pool_w.astype(BF16), pool_scale.reshape(1, pw))


def _gla_kernel(q_ref, k_ref, v_ref, g_ref, ps_ref, wup_ref, bgk_ref, gain_ref,
                tri_ref, ones_ref, seg_ref, o_ref,
                st_ref, qin_s, kdec_s, b_s, dl_s, qf_s, kf_s, o_s):
    rows = q_ref.shape[1]
    hk = GLA_HEADS * GLA_DK

    @pl.when(pl.program_id(1) == 0)
    def _():
        st_ref[...] = jnp.zeros_like(st_ref)

    low = ps_ref[0][:, :GLA_RANK]
    wup = wup_ref[...]
    x = jnp.broadcast_to(bgk_ref[...], (rows, hk))
    for r in range(GLA_RANK):
        x = x + low[:, r:r + 1] * wup[r:r + 1, :]
    la = (jnp.minimum(x, 0.0) - jnp.log1p(jnp.exp(-jnp.abs(x)))) * (1.0 / GLA_GATE_NORM)
    pieces = _split3(la)
    tri = tri_ref[...]
    ones = ones_ref[...]
    bcum = _dot(tri, pieces[0]) + _dot(tri, pieces[1]) + _dot(tri, pieces[2])
    tot = _dot(ones, pieces[0]) + _dot(ones, pieces[1]) + _dot(ones, pieces[2])
    q = q_ref[0].astype(F32) * (GLA_DK ** -0.5)
    k = k_ref[0].astype(F32)
    qf_s[...] = q
    kf_s[...] = k
    b_s[...] = bcum
    qin_s[...] = (q * jnp.exp(bcum)).astype(BF16)
    kdec_s[...] = (k * jnp.exp(tot - bcum)).astype(BF16)
    dl_s[...] = jnp.exp(tot)

    sub_i = lax.broadcasted_iota(jnp.int32, (GLA_SUB, 1), 0)
    seg = seg_ref[...]

    def step(n, carry):
        r0 = pl.multiple_of(n * GLA_SUB, GLA_SUB)
        rs = pl.ds(r0, GLA_SUB)
        bn = b_s[rs, :]
        qn = qf_s[rs, :]
        kn = kf_s[rs, :]
        vn = v_ref[0, rs, :]
        vf = vn.astype(F32)
        ws = []
        for j in range(GLA_SUB):
            e = jnp.exp(jnp.where(sub_i >= j, bn - bn[j:j + 1, :], -jnp.inf))
            ws.append((qn * (kn[j:j + 1, :] * e)).astype(BF16))
        z = _dot(jnp.concatenate(ws, axis=0), seg)
        intra = z[0:GLA_SUB, :] * vf[0:1, :]
        for j in range(1, GLA_SUB):
            intra = intra + z[j * GLA_SUB:(j + 1) * GLA_SUB, :] * vf[j:j + 1, :]
        qin = qin_s[rs, :]
        kdec = kdec_s[rs, :]
        dl = dl_s[pl.ds(r0, 1), :]
        st = st_ref[...]
        stb = st.astype(BF16)
        st_new = st * dl
        for h in range(GLA_HEADS):
            ck = slice(h * GLA_DK, (h + 1) * GLA_DK)
            cv = slice(h * GLA_DV, (h + 1) * GLA_DV)
            inter = lax.dot_general(qin[:, ck], stb[:, ck], _NT, preferred_element_type=F32)
            o_s[rs, cv] = inter + intra[:, cv]
            upd = lax.dot_general(vn[:, cv], kdec[:, ck], _TN, preferred_element_type=F32)
            st_ref[:, ck] = st_new[:, ck] + upd
        return carry

    lax.fori_loop(0, rows // GLA_SUB, step, 0)

    gain = gain_ref[...]
    for h in range(GLA_HEADS):
        cv = slice(h * GLA_DV, (h + 1) * GLA_DV)
        o = o_s[:, cv]
        y = o * lax.rsqrt(jnp.mean(o * o, axis=-1, keepdims=True) + EPS) * gain
        o_ref[0, :, cv] = (y * _silu(g_ref[0, :, cv].astype(F32))).astype(o_ref.dtype)


def _gla_call(proj, small, w_up, b_gk, gain):
    nb, s, _ = proj.shape
    rows = min(GLA_ROWS, s)
    hk, hv = GLA_HEADS * GLA_DK, GLA_HEADS * GLA_DV
    idx = np.arange(rows)
    same = (idx[:, None] // GLA_SUB) == (idx[None, :] // GLA_SUB)
    tri = jnp.asarray(same & (idx[None, :] <= idx[:, None]), BF16)
    ones = jnp.asarray(same, BF16)
    seg = jnp.asarray((np.arange(hk)[:, None] // GLA_DK) == (np.arange(hv)[None, :] // GLA_DV), BF16)
    const = lambda shape: pl.BlockSpec(shape, lambda b, i: (0,) * len(shape))
    return pl.pallas_call(
        _gla_kernel,
        grid=(nb, s // rows),
        in_specs=[pl.BlockSpec((1, rows, hk), lambda b, i: (b, i, 2)),
                  pl.BlockSpec((1, rows, hk), lambda b, i: (b, i, 3)),
                  pl.BlockSpec((1, rows, hv), lambda b, i: (b, i, 2)),
                  pl.BlockSpec((1, rows, hv), lambda b, i: (b, i, 3)),
                  pl.BlockSpec((1, rows, LANES), lambda b, i: (b, i, 0)),
                  const((GLA_RANK, hk)), const((1, hk)), const((1, GLA_DV)),
                  const((rows, rows)), const((rows, rows)), const((hk, hv))],
        out_specs=pl.BlockSpec((1, rows, hv), lambda b, i: (b, i, 0)),
        out_shape=jax.ShapeDtypeStruct((nb, s, hv), BF16),
        scratch_shapes=[pltpu.VMEM((GLA_DV, hk), F32),
                        pltpu.VMEM((rows, hk), BF16), pltpu.VMEM((rows, hk), BF16),
                        pltpu.VMEM((rows, hk), F32), pltpu.VMEM((rows, hk), F32),
                        pltpu.VMEM((rows, hk), F32), pltpu.VMEM((rows, hk), F32),
                        pltpu.VMEM((rows, hv), F32)],
        compiler_params=_cparams("parallel", "arbitrary"),
        name="gla",
    )(proj, proj, proj, proj, small, w_up, b_gk.reshape(1, hk), gain.reshape(1, GLA_DV), tri, ones, seg)


def _head_norm(x, gain):
    return x * lax.rsqrt(jnp.mean(x * x, axis=-1, keepdims=True) + EPS) * gain


def _nsa_prep_kernel(q_ref, ks_ref, vs_ref, kw_ref, vw_ref, ps_ref, qg_ref, kg_ref,
                     q2_ref, ksa_ref, kwp_ref, vst_ref, vwt_ref, gt_ref):
    ts = q_ref.shape[1]
    nblk = LANES - NSA_DK
    zeros = jnp.zeros((ts, NSA_DK), BF16)
    qg = qg_ref[...]
    kg = kg_ref[...]
    for h in range(NSA_HEADS):
        c = slice(h * NSA_DK, (h + 1) * NSA_DK)
        qn = _head_norm(q_ref[0, :, c].astype(F32), qg) * (NSA_DK ** -0.5 * LOG2E)
        q2_ref[0, h // NSA_REP, h % NSA_REP, :, 0:NSA_DK] = qn.astype(BF16)
        q2_ref[0, h // NSA_REP, h % NSA_REP, :, NSA_DK:] = jnp.zeros((ts, q2_ref.shape[4] - NSA_DK), BF16)
    pos = pl.program_id(1) * ts + lax.broadcasted_iota(jnp.int32, (ts, nblk), 0)
    blk = lax.broadcasted_iota(jnp.int32, (ts, nblk), 1)
    onehot = (jnp.right_shift(pos, int(math.log2(SLC_BLOCK))) == blk).astype(BF16)
    gates_t = jax.nn.sigmoid(ps_ref[0]).T
    vs_t = vs_ref[0].astype(F32).T
    vw_t = vw_ref[0].astype(F32).T
    gw = 3 * NSA_REP
    for g in range(NSA_GROUPS):
        c = slice(g * NSA_DK, (g + 1) * NSA_DK)
        ksa_ref[0, g, :, 0:NSA_DK] = _head_norm(ks_ref[0, :, c].astype(F32), kg[1:2, :]).astype(BF16)
        ksa_ref[0, g, :, NSA_DK:] = onehot
        kwp_ref[0, g, :, 0:NSA_DK] = _head_norm(kw_ref[0, :, c].astype(F32), kg[2:3, :]).astype(BF16)
        kwp_ref[0, g, :, NSA_DK:] = zeros
        vst_ref[0, g] = vs_t[c, :].astype(BF16)
        vwt_ref[0, g] = vw_t[c, :].astype(BF16)
        gt_ref[0, g, 0:gw, :] = gates_t[GLA_RANK + g * gw:GLA_RANK + (g + 1) * gw, :]
        gt_ref[0, g, gw:, :] = jnp.zeros((gt_ref.shape[2] - gw, ts), F32)


def _nsa_prep_call(proj, small, q_gain, k_gain, ts=512):
    nb, s, _ = proj.shape
    ts = min(ts, s)
    assert s // SLC_BLOCK <= LANES - NSA_DK
    kvw = NSA_GROUPS * NSA_DK
    qw = NSA_HEADS * NSA_DK
    col = lambda j: (lambda b, i: (b, i, j))
    grp = lambda w: pl.BlockSpec((1, NSA_GROUPS, ts, w), lambda b, i: (b, 0, i, 0))
    grp_t = lambda rows: pl.BlockSpec((1, NSA_GROUPS, rows, ts), lambda b, i: (b, 0, 0, i))
    gate_rows = 16
    return pl.pallas_call(
        _nsa_prep_kernel,
        grid=(nb, s // ts),
        in_specs=[pl.BlockSpec((1, ts, qw), col(2)),
                  pl.BlockSpec((1, ts, kvw), col(14)),
                  pl.BlockSpec((1, ts, kvw), col(15)),
                  pl.BlockSpec((1, ts, kvw), col(16)),
                  pl.BlockSpec((1, ts, kvw), col(17)),
                  pl.BlockSpec((1, ts, LANES), col(0)),
                  pl.BlockSpec((1, NSA_DK), lambda b, i: (0, 0)),
                  pl.BlockSpec((3, NSA_DK), lambda b, i: (0, 0))],
        out_specs=[pl.BlockSpec((1, NSA_GROUPS, NSA_REP, ts, LANES), lambda b, i: (b, 0, 0, i, 0)),
                   grp(LANES), grp(LANES), grp_t(NSA_DK), grp_t(NSA_DK), grp_t(gate_rows)],
        out_shape=[jax.ShapeDtypeStruct((nb, NSA_GROUPS, NSA_REP, s, LANES), BF16),
                   jax.ShapeDtypeStruct((nb, NSA_GROUPS, s, LANES), BF16),
                   jax.ShapeDtypeStruct((nb, NSA_GROUPS, s, LANES), BF16),
                   jax.ShapeDtypeStruct((nb, NSA_GROUPS, NSA_DK, s), BF16),
                   jax.ShapeDtypeStruct((nb, NSA_GROUPS, NSA_DK, s), BF16),
                   jax.ShapeDtypeStruct((nb, NSA_GROUPS, gate_rows, s), F32)],
        compiler_params=_cparams("parallel", "parallel"),
        name="nsa_prep",
    )(proj, proj, proj, proj, proj, small, q_gain.reshape(1, NSA_DK), k_gain)


def _nsa_compress_kernel(hk_ref, hv_ref, pos_ref, w1_ref, w2_ref, kg_ref, kc_ref, vc_ref):
    nh = hk_ref.shape[2]
    half = CMP_STRIDE * NSA_DK
    row = lax.broadcasted_iota(jnp.int32, (nh, 1), 0)
    for kv, (h_ref, o_ref) in enumerate(((hk_ref, kc_ref), (hv_ref, vc_ref))):
        hb = h_ref[0, 0].astype(F32)
        top = _dot((hb + pos_ref[kv, 0:1, :]).astype(BF16), w1_ref[kv, 0:half, :])
        bot = _dot((hb + pos_ref[kv, 1:2, :]).astype(BF16), w1_ref[kv, half:, :])
        hid = _silu(top + pltpu.roll(bot, nh - 1, axis=0))
        out = _dot(hid.astype(BF16), w2_ref[kv])
        if kv == 0:
            out = _head_norm(out, kg_ref[...])
        out = jnp.where(row < nh - 1, out, 0.0)
        if kv == 0:
            o_ref[0, 0, :, 0:NSA_DK] = out.astype(BF16)
            o_ref[0, 0, :, NSA_DK:] = jnp.zeros((nh, LANES - NSA_DK), BF16)
        else:
            o_ref[0, 0] = out.T.astype(BF16)


def _nsa_compress_call(hk, hv, pos, w1, w2, kgain0):
    nb, ng, nh, hw = hk.shape
    hid = w1.shape[2]
    blk = pl.BlockSpec((1, 1, nh, hw), lambda b, g: (b, g, 0, 0))
    out = pl.BlockSpec((1, 1, nh, LANES), lambda b, g: (b, g, 0, 0))
    return pl.pallas_call(
        _nsa_compress_kernel,
        grid=(nb, ng),
        in_specs=[blk, blk,
                  pl.BlockSpec((2, 2, hw), lambda b, g: (0, 0, 0)),
                  pl.BlockSpec((2, 2 * hw, hid), lambda b, g: (0, 0, 0)),
                  pl.BlockSpec((2, hid, NSA_DK), lambda b, g: (0, 0, 0)),
                  pl.BlockSpec((1, NSA_DK), lambda b, g: (0, 0))],
        out_specs=[out, pl.BlockSpec((1, 1, NSA_DK, nh), lambda b, g: (b, g, 0, 0))],
        out_shape=[jax.ShapeDtypeStruct((nb, ng, nh, LANES), BF16),
                   jax.ShapeDtypeStruct((nb, ng, NSA_DK, nh), BF16)],
        compiler_params=_cparams("parallel", "parallel"),
        name="nsa_compress",
    )(hk, hv, pos.reshape(2, 2, hw), w1.astype(BF16), w2.astype(BF16), kgain0.reshape(1, NSA_DK))


def _softmax_cols(s):
    m = jnp.max(s, axis=0, keepdims=True)
    m = jnp.where(m > -jnp.inf, m, 0.0)
    e = jnp.exp2(s - m)
    d = jnp.sum(e, axis=0, keepdims=True)
    return e, 1.0 / jnp.where(d > 0, d, 1.0)


def _query_lanes(slopes_ref, g, q0, tq):
    r = NSA_REP * tq
    lane = lax.broadcasted_iota(jnp.int32, (1, r), 1)
    rep = jnp.zeros((1, r), jnp.int32)
    for i in range(1, NSA_REP):
        rep = rep + (lane >= i * tq).astype(jnp.int32)
    slope = jnp.zeros((1, r), F32)
    for i in range(NSA_REP):
        slope = jnp.where(rep == i, slopes_ref[g * NSA_REP + i], slope)
    return q0 + (lane - rep * tq), slope


def _nsa_select_kernel(slopes_ref, q_ref, kc_ref, vct_ref, mt_ref, psel_ref, qs_ref, oc_ref):
    tq = q_ref.shape[3]
    r = NSA_REP * tq
    nc = kc_ref.shape[2]
    nblk = mt_ref.shape[0]
    g = pl.program_id(1)
    q0 = pl.program_id(2) * tq
    q2 = q_ref[0, 0].reshape(r, LANES)
    t_q, slope = _query_lanes(slopes_ref, g, q0, tq)

    s_c = lax.dot_general(kc_ref[0, 0], q2, _NT, preferred_element_type=F32)
    cend = lax.broadcasted_iota(jnp.int32, (nc, r), 0) * CMP_STRIDE + (CMP_BLOCK - 1)
    dist_c = t_q - cend
    e_c, inv_c = _softmax_cols(jnp.where(dist_c >= 0, s_c - slope * dist_c.astype(F32), -jnp.inf))
    o_c = _dot(vct_ref[0, 0], e_c.astype(BF16)) * inv_c
    for i in range(NSA_REP):
        oc_ref[0, 0, i] = o_c[:, i * tq:(i + 1) * tq].astype(oc_ref.dtype)

    p_sum = e_c[:, 0:tq] * inv_c[:, 0:tq]
    for i in range(1, NSA_REP):
        p_sum = p_sum + e_c[:, i * tq:(i + 1) * tq] * inv_c[:, i * tq:(i + 1) * tq]
    mt = mt_ref[...]
    imp = sum(_dot(mt, piece) for piece in _split3(p_sum))
    blk = lax.broadcasted_iota(jnp.int32, (nblk, 1), 0).astype(F32)
    t_row = q0 + lax.broadcasted_iota(jnp.int32, (1, tq), 1)
    cur = jnp.right_shift(t_row, int(math.log2(SLC_BLOCK))).astype(F32)
    forced = (blk == 0.0) | (blk == cur) | (blk == cur - 1.0)
    valid = blk <= cur
    score = jnp.where(valid, imp + FORCE_BONUS * forced.astype(F32), -jnp.inf)
    sel = jnp.zeros((nblk, tq), F32)
    for _ in range(min(N_SELECT, nblk)):
        best = jnp.max(score, axis=0, keepdims=True)
        first = jnp.min(jnp.where(score == best, blk, float(nblk)), axis=0, keepdims=True)
        pick = blk == first
        sel = jnp.where(pick, 1.0, sel)
        score = jnp.where(pick, -jnp.inf, score)
    sel_bias = jnp.where((sel > 0.0) & valid, 0.0, -MASK_BIG).astype(BF16)
    bias_q = lax.dot_general(sel_bias, psel_ref[...], _TN, preferred_element_type=F32)
    bias_q = bias_q.astype(BF16)
    for i in range(NSA_REP):
        qs_ref[0, 0, i] = q_ref[0, 0, i] + bias_q


def _nsa_attn_kernel(slopes_ref, q_ref, qs_ref, oc_ref, ksa_ref, kwp_ref, vst_ref, vwt_ref, gt_ref, wb_ref,
                     o_ref, sa0_ref, sa1_ref, sp0_ref, sp1_ref):
    tq = q_ref.shape[3]
    r = NSA_REP * tq
    seq = ksa_ref.shape[2]
    g = pl.program_id(1)
    q0 = pl.program_id(2) * tq
    q2 = q_ref[0, 0].reshape(r, LANES)
    q_sel = qs_ref[0, 0].reshape(r, LANES)
    t_q, slope = _query_lanes(slopes_ref, g, q0, tq)

    tks = min(NSA_TKS, seq)
    pos0 = lax.broadcasted_iota(jnp.int32, (tks, r), 0)
    sb = slope * pos0.astype(F32)

    def sel_scores(j):
        k0 = pl.multiple_of(j * tks, tks)
        return lax.dot_general(ksa_ref[0, 0, pl.ds(k0, tks), :], q_sel, _NT, preferred_element_type=F32) + sb

    def sel_values(j, p):
        return _dot(vst_ref[0, 0, :, pl.ds(pl.multiple_of(j * tks, tks), tks)], p)

    def sel_softmax(j, a, m, l):
        c = slope * (j * tks + jnp.zeros((1, r), jnp.int32)).astype(F32)
        m_new = jnp.maximum(m, jnp.max(a, axis=0, keepdims=True) + c)
        m_safe = jnp.where(m_new > -jnp.inf, m_new, 0.0)
        p = jnp.exp2(a - (m_safe - c))
        alpha = jnp.exp2(m - m_safe)
        return p.astype(BF16), alpha, m_new, alpha * l + jnp.sum(p, axis=0, keepdims=True)

    def sel_step(j, a_cur, a_next, p_prev, p_cur, carry):
        m, l, acc = carry
        a_next[...] = sel_scores(j + 1)
        acc = acc + sel_values(jnp.maximum(j - 1, 0), p_prev[...])
        p, alpha, m, l = sel_softmax(j, a_cur[...], m, l)
        p_cur[...] = p
        return m, l, alpha * acc

    def sel_pair(i, carry):
        carry = sel_step(2 * i, sa0_ref, sa1_ref, sp1_ref, sp0_ref, carry)
        return sel_step(2 * i + 1, sa1_ref, sa0_ref, sp0_ref, sp1_ref, carry)

    def sel_finish(a_ref, p_ref, carry):
        m, l, acc = carry
        acc = acc + sel_values(jnp.maximum(n_big - 1, 0), p_ref[...])
        a_d = jnp.where(n_big * tks + pos0 <= t_q, a_ref[...], -jnp.inf)
        p_d, alpha_d, _, l = sel_softmax(n_big, a_d, m, l)
        acc = alpha_d * acc + sel_values(n_big, p_d)
        return acc * (1.0 / jnp.where(l > 0, l, 1.0))

    n_big = q0 // tks
    sa0_ref[...] = sel_scores(0)
    sp1_ref[...] = jnp.zeros((tks, r), BF16)
    carry = (jnp.full((1, r), -jnp.inf, F32), jnp.zeros((1, r), F32), jnp.zeros((NSA_DK, r), F32))
    carry = lax.fori_loop(0, n_big // 2, sel_pair, carry)
    o_s = lax.cond(n_big % 2 == 1,
                   lambda cr: sel_finish(sa1_ref, sp0_ref, sel_step(n_big - 1, sa0_ref, sa1_ref, sp1_ref, sp0_ref, cr)),
                   lambda cr: sel_finish(sa0_ref, sp1_ref, cr), carry)

    span = WINDOW + tq

    def window(start, bias_fn):
        s_w = lax.dot_general(kwp_ref[0, 0, pl.ds(start, span), :], q2, _NT, preferred_element_type=F32)
        e_w, inv_w = _softmax_cols(bias_fn(s_w))
        return _dot(vwt_ref[0, 0, :, pl.ds(start, span)], e_w.astype(BF16)) * inv_w

    def window_head(s_w):
        dist_w = t_q - lax.broadcasted_iota(jnp.int32, (span, r), 0)
        return jnp.where((dist_w >= 0) & (dist_w < WINDOW), s_w - slope * dist_w.astype(F32), -jnp.inf)

    o_w = lax.cond(q0 >= WINDOW,
                   lambda: window(pl.multiple_of(q0 - WINDOW, tq), lambda s_w: s_w + wb_ref[0]),
                   lambda: window(0, window_head))

    gt = gt_ref[0, 0]
    outs = []
    for i in range(NSA_REP):
        ls = slice(i * tq, (i + 1) * tq)
        outs.append(gt[3 * i:3 * i + 1, :] * oc_ref[0, 0, i].astype(F32) + gt[3 * i + 1:3 * i + 2, :] * o_s[:, ls]
                    + gt[3 * i + 2:3 * i + 3, :] * o_w[:, ls])
    o_ref[0] = jnp.concatenate(outs, axis=0).T.astype(o_ref.dtype)


def _alibi_slopes(n):
    return np.asarray([2.0 ** (-8.0 * (i + 1) / n) for i in range(n)], dtype=np.float32)


def _cmp_to_slc_t(s, nc_pad):
    n_cmp = (s - CMP_BLOCK) // CMP_STRIDE + 1
    cs = np.arange(n_cmp) * CMP_STRIDE
    ss = np.arange(s // SLC_BLOCK) * SLC_BLOCK
    ov = np.clip(np.minimum(cs[:, None] + CMP_BLOCK, ss[None, :] + SLC_BLOCK)
                 - np.maximum(cs[:, None], ss[None, :]), 0, None)
    m = np.zeros((nc_pad, s // SLC_BLOCK), np.float32)
    m[:n_cmp] = ov / CMP_STRIDE
    return m.T


def _nsa_select_call(q2, kc, vct):
    nb, ng, nrep, s, _ = q2.shape
    tq = min(NSA_TSEL, s)
    nc = kc.shape[2]
    nblk = s // SLC_BLOCK
    mt = jnp.asarray(_cmp_to_slc_t(s, nc), BF16)
    psel = np.zeros((nblk, LANES), np.float32)
    psel[np.arange(nblk), NSA_DK + np.arange(nblk)] = 1.0
    slopes = _alibi_slopes(NSA_HEADS).astype(np.float64) * LOG2E
    q_spec = pl.BlockSpec((1, 1, nrep, tq, LANES), lambda b, g, i: (b, g, 0, i, 0))
    return pl.pallas_call(
        _nsa_select_kernel,
        grid=(nb, ng, s // tq),
        in_specs=[pl.BlockSpec(memory_space=pltpu.SMEM), q_spec,
                  pl.BlockSpec((1, 1, nc, LANES), lambda b, g, i: (b, g, 0, 0)),
                  pl.BlockSpec((1, 1, NSA_DK, nc), lambda b, g, i: (b, g, 0, 0)),
                  pl.BlockSpec((nblk, nc), lambda b, g, i: (0, 0)),
                  pl.BlockSpec((nblk, LANES), lambda b, g, i: (0, 0))],
        out_specs=[q_spec, pl.BlockSpec((1, 1, nrep, NSA_DK, tq), lambda b, g, i: (b, g, 0, 0, i))],
        out_shape=[jax.ShapeDtypeStruct(q2.shape, BF16),
                   jax.ShapeDtypeStruct((nb, ng, nrep, NSA_DK, s), BF16)],
        compiler_params=_cparams("parallel", "parallel", "arbitrary"),
        name="nsa_select",
    )(jnp.asarray(slopes, F32), q2, kc, vct, mt, jnp.asarray(psel, BF16))


def _nsa_attn_call(q2, q_sel, oc, ksa, kwp, vst, vwt, gt):
    nb, ng, nrep, s, _ = q2.shape
    tq = min(NSA_TQ, s)
    tks = min(NSA_TKS, s)
    assert s >= WINDOW + tq and WINDOW % tq == 0
    slopes = _alibi_slopes(NSA_HEADS).astype(np.float64) * LOG2E
    dist = WINDOW + np.arange(tq)[None, :] - np.arange(WINDOW + tq)[:, None]
    wbias = np.where((dist >= 0) & (dist < WINDOW), -slopes[:, None, None] * dist[None], -np.inf)
    wbias = wbias.reshape(ng, nrep, WINDOW + tq, tq).transpose(0, 2, 1, 3).reshape(ng, WINDOW + tq, nrep * tq)
    seq = pl.BlockSpec((1, 1, s, LANES), lambda b, g, i: (b, g, 0, 0))
    seq_t = pl.BlockSpec((1, 1, NSA_DK, s), lambda b, g, i: (b, g, 0, 0))
    q_spec = pl.BlockSpec((1, 1, nrep, tq, LANES), lambda b, g, i: (b, g, 0, i, 0))
    return pl.pallas_call(
        _nsa_attn_kernel,
        grid=(nb, ng, s // tq),
        in_specs=[pl.BlockSpec(memory_space=pltpu.SMEM), q_spec, q_spec,
                  pl.BlockSpec((1, 1, nrep, NSA_DK, tq), lambda b, g, i: (b, g, 0, 0, i)),
                  seq, seq, seq_t, seq_t,
                  pl.BlockSpec((1, 1, gt.shape[2], tq), lambda b, g, i: (b, g, 0, i)),
                  pl.BlockSpec((1, WINDOW + tq, nrep * tq), lambda b, g, i: (g, 0, 0))],
        out_specs=pl.BlockSpec((1, tq, nrep * NSA_DK), lambda b, g, i: (b, i, g)),
        out_shape=jax.ShapeDtypeStruct((nb, s, ng * nrep * NSA_DK), BF16),
        scratch_shapes=[pltpu.VMEM((tks, nrep * tq), F32), pltpu.VMEM((tks, nrep * tq), F32),
                        pltpu.VMEM((tks, nrep * tq), BF16), pltpu.VMEM((tks, nrep * tq), BF16)],
        compiler_params=_cparams("parallel", "parallel", "arbitrary"),
        name="nsa_attention",
    )(jnp.asarray(slopes, F32), q2, q_sel, oc, ksa, kwp, vst, vwt, gt, jnp.asarray(wbias, F32))


def _out_proj_kernel(a0_ref, a1_ref, a2_ref, w0_ref, w1_ref, w2_ref, x_ref, g_ref, o_ref):
    acc = _dot(a0_ref[...], w0_ref[...]) + _dot(a1_ref[...], w1_ref[...]) + _dot(a2_ref[...], w2_ref[...])
    o_ref[...] = x_ref[...] + g_ref[0] * acc


def _out_proj_call(pool_o, gla_o, nsa_o, w_out, x2d, gate, seq, tm=1024, tn=1024):
    m, d = x2d.shape
    tm = min(tm, seq)
    k0, k1, k2 = pool_o.shape[1], gla_o.shape[1], nsa_o.shape[1]
    assert k0 == k1 and k2 == 2 * k0
    per_b = seq // tm
    a = lambda k: pl.BlockSpec((tm, k), lambda j, i: (i, 0))
    return pl.pallas_call(
        _out_proj_kernel,
        grid=(d // tn, m // tm),
        in_specs=[a(k0), a(k1), a(k2),
                  pl.BlockSpec((k0, tn), lambda j, i: (0, j)),
                  pl.BlockSpec((k1, tn), lambda j, i: (1, j)),
                  pl.BlockSpec((k2, tn), lambda j, i: (1, j)),
                  pl.BlockSpec((tm, tn), lambda j, i: (i, j)),
                  pl.BlockSpec((1, 1, tn), lambda j, i: (i // per_b, 0, j))],
        out_specs=pl.BlockSpec((tm, tn), lambda j, i: (i, j)),
        out_shape=jax.ShapeDtypeStruct((m, d), F32),
        compiler_params=_cparams("parallel", "parallel"),
        name="out_proj_residual",
    )(pool_o, gla_o, nsa_o, w_out, w_out, w_out, x2d, gate.reshape(-1, 1, d))


def _ffn_kernel(h_ref, wg_ref, wu_ref, wo_ref, x_ref, g_ref, o_ref, acc_ref):
    j = pl.program_id(1)

    @pl.when(j == 0)
    def _():
        acc_ref[...] = jnp.zeros_like(acc_ref)

    h = h_ref[...]
    act = (_silu(_dot(h, wg_ref[...])) * _dot(h, wu_ref[...])).astype(BF16)
    acc_ref[...] += _dot(act, wo_ref[...])

    @pl.when(j == pl.num_programs(1) - 1)
    def _():
        o_ref[...] = x_ref[...] + g_ref[0] * acc_ref[...]


def _ffn_call(h2d, w_in, w_out, x2d, gate, seq, tm=512, th=512):
    m, d = x2d.shape
    tm = min(tm, seq)
    hid = w_out.shape[0]
    nh = hid // th
    per_b = seq // tm
    return pl.pallas_call(
        _ffn_kernel,
        grid=(m // tm, nh),
        in_specs=[pl.BlockSpec((tm, d), lambda i, j: (i, 0)),
                  pl.BlockSpec((d, th), lambda i, j: (0, j)),
                  pl.BlockSpec((d, th), lambda i, j: (0, j + nh)),
                  pl.BlockSpec((th, d), lambda i, j: (j, 0)),
                  pl.BlockSpec((tm, d), lambda i, j: (i, 0)),
                  pl.BlockSpec((1, 1, d), lambda i, j: (i // per_b, 0, 0))],
        out_specs=pl.BlockSpec((tm, d), lambda i, j: (i, 0)),
        out_shape=jax.ShapeDtypeStruct((m, d), F32),
        scratch_shapes=[pltpu.VMEM((tm, d), F32)],
        compiler_params=_cparams("parallel", "arbitrary"),
        name="ffn_swiglu_residual",
    )(h2d, w_in, w_in, w_out, x2d, gate.reshape(-1, 1, d))


def _regroup_w_in(w):
    d = w.shape[0]
    a = 4 * POOL_GROUP + 2 * GLA_HEADS * GLA_DK + 2 * GLA_HEADS * GLA_DV
    b = a + GLA_RANK
    c = b + NSA_HEADS * NSA_DK + 6 * NSA_GROUPS * NSA_DK
    ng = 3 * NSA_HEADS
    main = jnp.concatenate([w[:, :a], w[:, b:c]], axis=1).astype(BF16)
    small = jnp.concatenate([w[:, a:b], w[:, c:c + ng], jnp.zeros((d, LANES - GLA_RANK - ng), w.dtype)],
                            axis=1).astype(BF16)
    return main, small


def _layer(x, mod, p):
    nb, s, d = x.shape
    sh_a, sc_a, g_a, sh_f, sc_f, g_f = [mod[:, i * d:(i + 1) * d] for i in range(6)]

    hmix = _norm_mod_call(x, p["norm_mix_gain"], sh_a, sc_a).reshape(nb * s, d)
    w_main, w_small = _regroup_w_in(p["w_in"])
    proj = _matmul_call(hmix, w_main, BF16, min(1024, s), 1536, "in_proj").reshape(nb, s, -1)
    small = _matmul_call(hmix, w_small, F32, min(1024, s), LANES, "in_proj_small").reshape(nb, s, LANES)

    pool_o = _pool_call(proj, p["pool_w"], p["pool_scale"])
    gla_o = _gla_call(proj, small, p["gla_w_gk_up"], p["gla_b_gk"], p["gla_norm_gain"])

    q2, ksa, kwp, vst, vwt, gt = _nsa_prep_call(proj, small, p["nsa_q_gain"], p["nsa_k_gain"])
    kv0 = 4 * POOL_GROUP + 2 * GLA_HEADS * (GLA_DK + GLA_DV) + NSA_HEADS * NSA_DK
    kvw = NSA_GROUPS * NSA_DK

    def half_blocks(cols):
        t = cols.reshape(nb, s // CMP_STRIDE, CMP_STRIDE, NSA_GROUPS, NSA_DK)
        return jnp.transpose(t, (0, 3, 1, 2, 4)).reshape(nb, NSA_GROUPS, s // CMP_STRIDE, CMP_STRIDE * NSA_DK)

    kc, vc = _nsa_compress_call(half_blocks(proj[:, :, kv0:kv0 + kvw]),
                                half_blocks(proj[:, :, kv0 + kvw:kv0 + 2 * kvw]),
                                p["nsa_cmp_pos"], p["nsa_cmp_w1"], p["nsa_cmp_w2"], p["nsa_k_gain"][0])
    q_sel, oc = _nsa_select_call(q2, kc, vc)
    nsa_o = _nsa_attn_call(q2, q_sel, oc, ksa, kwp, vst, vwt, gt)

    x2d = x.reshape(nb * s, d)
    x1 = _out_proj_call(pool_o.reshape(nb * s, -1), gla_o.reshape(nb * s, -1), nsa_o.reshape(nb * s, -1),
                        p["w_out"].astype(BF16), x2d, g_a, s)

    hf = _norm_mod_call(x1.reshape(nb, s, d), p["norm_ffn_gain"], sh_f, sc_f).reshape(nb * s, d)
    x2 = _ffn_call(hf, p["w_ffn_in"].astype(BF16), p["w_ffn_out"].astype(BF16), x1, g_f, s)
    return x2.reshape(nb, s, d)


_PER_LAYER = ("norm_mix_gain", "norm_ffn_gain", "w_in", "w_out", "pool_w", "pool_scale", "gla_w_gk_up",
              "gla_b_gk", "gla_norm_gain", "nsa_q_gain", "nsa_k_gain", "nsa_cmp_pos", "nsa_cmp_w1",
              "nsa_cmp_w2", "w_ffn_in", "w_ffn_out")


def kernel(x, c, norm_mix_gain, norm_ffn_gain, w_mod, b_mod, w_in, w_out, pool_w, pool_scale, gla_w_gk_up,
           gla_b_gk, gla_norm_gain, nsa_q_gain, nsa_k_gain, nsa_cmp_pos, nsa_cmp_w1, nsa_cmp_w2, w_ffn_in,
           w_ffn_out):
    params = dict(norm_mix_gain=norm_mix_gain, norm_ffn_gain=norm_ffn_gain, w_in=w_in, w_out=w_out,
                  pool_w=pool_w, pool_scale=pool_scale, gla_w_gk_up=gla_w_gk_up, gla_b_gk=gla_b_gk,
                  gla_norm_gain=gla_norm_gain, nsa_q_gain=nsa_q_gain, nsa_k_gain=nsa_k_gain,
                  nsa_cmp_pos=nsa_cmp_pos, nsa_cmp_w1=nsa_cmp_w1, nsa_cmp_w2=nsa_cmp_w2,
                  w_ffn_in=w_ffn_in, w_ffn_out=w_ffn_out)
    mod = _mod_call(c, w_mod, b_mod)
    for l in range(w_mod.shape[0]):
        x = _layer(x, mod[l], {k: params[k][l] for k in _PER_LAYER})
    return x
```

```python
import functools
import math

import jax
import jax.numpy as jnp
import numpy as np
from jax import lax
from jax.experimental import pallas as pl
from jax.experimental.pallas import tpu as pltpu

F32 = jnp.float32
BF16 = jnp.bfloat16

POOL_WINDOWS = (2, 4, 8, 16)
POOL_GROUP = 128
GLA_HEADS = 4
GLA_DK = 64
GLA_DV = 128
GLA_RANK = 16
GLA_GATE_NORM = 16.0
NSA_HEADS = 16
NSA_GROUPS = 4
NSA_REP = NSA_HEADS // NSA_GROUPS
NSA_DK = 64
CMP_BLOCK = 32
CMP_STRIDE = 16
SLC_BLOCK = 64
N_SELECT = 16
WINDOW = 512
FORCE_BONUS = 100.0
EPS = 1e-6

LANES = 128
VMEM_LIMIT = 56 * 1024 * 1024

GLA_SUB = 16
GLA_ROWS = 256
NSA_TSEL = 512
NSA_TQ = 256
NSA_TKS = 256
NSA_TKW = 128
MASK_BIG = 2.0 ** 100
LOG2E = math.log2(math.e)

_NT = (((1,), (1,)), ((), ()))
_TN = (((0,), (0,)), ((), ()))


def _cparams(*sem):
    return pltpu.CompilerParams(dimension_semantics=sem, vmem_limit_bytes=VMEM_LIMIT)


def _dot(a, b):
    return jnp.dot(a, b, preferred_element_type=F32)


def _split3(x):
    hi = x.astype(BF16)
    r1 = x - hi.astype(F32)
    mid = r1.astype(BF16)
    lo = (r1 - mid.astype(F32)).astype(BF16)
    return hi, mid, lo


def _silu(x):
    return x * jax.nn.sigmoid(x)


def _mod_kernel(ct_ref, w_ref, b_ref, o_ref):
    ct = ct_ref[...]
    act = _silu(ct)
    w = w_ref[0]
    for b in range(ct.shape[1]):
        row = jnp.sum(act[:, b:b + 1] * w, axis=0, keepdims=True)
        o_ref[0, b:b + 1, :] = row + b_ref[0]


def _mod_call(c, w_mod, b_mod, tn=512):
    nl, d, n = w_mod.shape
    nb = c.shape[0]
    return pl.pallas_call(
        _mod_kernel,
        grid=(nl, n // tn),
        in_specs=[pl.BlockSpec((d, nb), lambda l, j: (0, 0)),
                  pl.BlockSpec((1, d, tn), lambda l, j: (l, 0, j)),
                  pl.BlockSpec((1, 1, tn), lambda l, j: (l, 0, j))],
        out_specs=pl.BlockSpec((1, nb, tn), lambda l, j: (l, 0, j)),
        out_shape=jax.ShapeDtypeStruct((nl, nb, n), F32),
        compiler_params=_cparams("parallel", "parallel"),
        name="adaln_mod",
    )(c.T, w_mod, b_mod.reshape(nl, 1, n))


def _norm_mod(x, gain, shift, scale):
    y = x * lax.rsqrt(jnp.mean(x * x, axis=-1, keepdims=True) + EPS) * gain
    return (y * (1.0 + scale) + shift).astype(BF16)


def _in_proj_kernel(x_ref, gn_ref, sh_ref, sc_ref, w_ref, ws_ref, o_ref, os_ref):
    h = _norm_mod(x_ref[...], gn_ref[...], sh_ref[0], sc_ref[0])
    o_ref[...] = _dot(h, w_ref[...]).astype(o_ref.dtype)
    os_ref[0] = _dot(h, ws_ref[...])


def _in_proj_call(x2d, gain, shift, scale, w_main, w_small, seq, tm=1024, tn=1536):
    m, d = x2d.shape
    tm = min(tm, seq)
    n = w_main.shape[1]
    per_b = seq // tm
    per_batch = pl.BlockSpec((1, 1, d), lambda j, i: (i // per_b, 0, 0))
    return pl.pallas_call(
        _in_proj_kernel,
        grid=(n // tn, m // tm),
        in_specs=[pl.BlockSpec((tm, d), lambda j, i: (i, 0)),
                  pl.BlockSpec((1, d), lambda j, i: (0, 0)), per_batch, per_batch,
                  pl.BlockSpec((d, tn), lambda j, i: (0, j)),
                  pl.BlockSpec((d, LANES), lambda j, i: (0, 0))],
        out_specs=[pl.BlockSpec((tm, tn), lambda j, i: (i, j)),
                   pl.BlockSpec((1, tm, LANES), lambda j, i: (j, i, 0))],
        out_shape=[jax.ShapeDtypeStruct((m, n), BF16), jax.ShapeDtypeStruct((n // tn, m, LANES), F32)],
        compiler_params=_cparams("parallel", "parallel"),
        name="norm_in_proj",
    )(x2d, gain.reshape(1, d), shift.reshape(-1, 1, d), scale.reshape(-1, 1, d), w_main, w_small)


def _pool_kernel(u_ref, w_ref, sc_ref, o_ref):
    s = u_ref.shape[1]
    t = lax.broadcasted_iota(jnp.int32, (s, 1), 0)
    for gi, win in enumerate(POOL_WINDOWS):
        cols = slice(gi * POOL_GROUP, (gi + 1) * POOL_GROUP)
        u = u_ref[0, :, cols].astype(F32)
        acc = u
        k = 1
        while k < win:
            acc = acc + jnp.where(t >= k, pltpu.roll(acc, k, axis=0), 0.0)
            k *= 2
        cnt = jnp.minimum(t + 1, win).astype(F32)
        dlt = (acc / cnt - u).astype(BF16)
        y = _dot(dlt, w_ref[gi])
        o_ref[0, :, cols] = (y * sc_ref[:, cols]).astype(o_ref.dtype)


def _pool_call(proj, pool_w, pool_scale):
    nb, s, _ = proj.shape
    pw = len(POOL_WINDOWS) * POOL_GROUP
    return pl.pallas_call(
        _pool_kernel,
        grid=(nb,),
        in_specs=[pl.BlockSpec((1, s, pw), lambda b: (b, 0, 0)),
                  pl.BlockSpec(pool_w.shape, lambda b: (0, 0, 0)),
                  pl.BlockSpec((1, pw), lambda b: (0, 0))],
        out_specs=pl.BlockSpec((1, s, pw), lambda b: (b, 0, 0)),
        out_shape=jax.ShapeDtypeStruct((nb, s, pw), BF16),
        compiler_params=_cparams("parallel"),
        name="pool_mixer",
    )(proj, pool_w.astype(BF16), pool_scale.reshape(1, pw))


def _gla_kernel(q_ref, k_ref, v_ref, g_ref, ps_ref, wup_ref, bgk_ref, gain_ref,
                tri_ref, ones_ref, seg_ref, o_ref,
                st_ref, qin_s, kdec_s, b_s, dl_s, qf_s, kf_s, o_s):
    rows = q_ref.shape[1]
    hk = GLA_HEADS * GLA_DK

    @pl.when(pl.program_id(1) == 0)
    def _():
        st_ref[...] = jnp.zeros_like(st_ref)

    low_hi, low_mid, _ = _split3(ps_ref[0])
    wup_hi, wup_mid, _ = _split3(wup_ref[...])
    x = bgk_ref[...] + (_dot(low_hi, wup_hi) + _dot(low_hi, wup_mid) + _dot(low_mid, wup_hi))
    la =(jnp.minimum(x, 0.0) - jnp.log1p(jnp.exp(-jnp.abs(x)))) * (1.0 / GLA_GATE_NORM)
    pieces = _split3(la)
    tri = tri_ref[...]
    ones = ones_ref[...]
    bcum = _dot(tri, pieces[0]) + _dot(tri, pieces[1]) + _dot(tri, pieces[2])
    tot = _dot(ones, pieces[0]) + _dot(ones, pieces[1]) + _dot(ones, pieces[2])
    q = q_ref[0].astype(F32) * (GLA_DK ** -0.5)
    k = k_ref[0].astype(F32)
    qf_s[...] = q
    kf_s[...] = k
    b_s[...] = bcum
    qin_s[...] = (q * jnp.exp(bcum)).astype(BF16)
    kdec_s[...] = (k * jnp.exp(tot - bcum)).astype(BF16)
    dl_s[...] = jnp.exp(tot)

    sub_i = lax.broadcasted_iota(jnp.int32, (GLA_SUB, 1), 0)
    seg = seg_ref[...]

    def step(n, carry):
        r0 = pl.multiple_of(n * GLA_SUB, GLA_SUB)
        rs = pl.ds(r0, GLA_SUB)
        bn = b_s[rs, :]
        qn = qf_s[rs, :]
        kn = kf_s[rs, :]
        vn = v_ref[0, rs, :]
        vf = vn.astype(F32)
        ws = []
        for j in range(GLA_SUB):
            e = jnp.exp(jnp.where(sub_i >= j, bn - bn[j:j + 1, :], -jnp.inf))
            ws.append((qn * (kn[j:j + 1, :] * e)).astype(BF16))
        z = _dot(jnp.concatenate(ws, axis=0), seg)
        intra = z[0:GLA_SUB, :] * vf[0:1, :]
        for j in range(1, GLA_SUB):
            intra = intra + z[j * GLA_SUB:(j + 1) * GLA_SUB, :] * vf[j:j + 1, :]
        qin = qin_s[rs, :]
        kdec = kdec_s[rs, :]
        dl = dl_s[pl.ds(r0, 1), :]
        st = st_ref[...]
        stb = st.astype(BF16)
        st_new = st * dl
        for h in range(GLA_HEADS):
            ck = slice(h * GLA_DK, (h + 1) * GLA_DK)
            cv = slice(h * GLA_DV, (h + 1) * GLA_DV)
            inter = lax.dot_general(qin[:, ck], stb[:, ck], _NT, preferred_element_type=F32)
            o_s[rs, cv] = inter + intra[:, cv]
            upd = lax.dot_general(vn[:, cv], kdec[:, ck], _TN, preferred_element_type=F32)
            st_ref[:, ck] = st_new[:, ck] + upd
        return carry

    lax.fori_loop(0, rows // GLA_SUB, step, 0)

    gain = gain_ref[...]
    for h in range(GLA_HEADS):
        cv = slice(h * GLA_DV, (h + 1) * GLA_DV)
        o = o_s[:, cv]
        y = o * lax.rsqrt(jnp.mean(o * o, axis=-1, keepdims=True) + EPS) * gain
        o_ref[0, :, cv] = (y * _silu(g_ref[0, :, cv].astype(F32))).astype(o_ref.dtype)


def _gla_call(proj, small, w_up, b_gk, gain):
    nb, s, _ = proj.shape
    rows = min(GLA_ROWS, s)
    hk, hv = GLA_HEADS * GLA_DK, GLA_HEADS * GLA_DV
    idx = np.arange(rows)
    same = (idx[:, None] // GLA_SUB) == (idx[None, :] // GLA_SUB)
    tri = jnp.asarray(same & (idx[None, :] <= idx[:, None]), BF16)
    ones = jnp.asarray(same, BF16)
    seg = jnp.asarray((np.arange(hk)[:, None] // GLA_DK) == (np.arange(hv)[None, :] // GLA_DV), BF16)
    const = lambda shape: pl.BlockSpec(shape, lambda b, i: (0,) * len(shape))
    return pl.pallas_call(
        _gla_kernel,
        grid=(nb, s // rows),
        in_specs=[pl.BlockSpec((1, rows, hk), lambda b, i: (b, i, 2)),
                  pl.BlockSpec((1, rows, hk), lambda b, i: (b, i, 3)),
                  pl.BlockSpec((1, rows, hv), lambda b, i: (b, i, 2)),
                  pl.BlockSpec((1, rows, hv), lambda b, i: (b, i, 3)),
                  pl.BlockSpec((1, rows, LANES), lambda b, i: (b, i, 0)),
                  const((LANES, hk)), const((1, hk)), const((1, GLA_DV)),
                  const((rows, rows)), const((rows, rows)), const((hk, hv))],
        out_specs=pl.BlockSpec((1, rows, hv), lambda b, i: (b, i, 0)),
        out_shape=jax.ShapeDtypeStruct((nb, s, hv), BF16),
        scratch_shapes=[pltpu.VMEM((GLA_DV, hk), F32),
                        pltpu.VMEM((rows, hk), BF16), pltpu.VMEM((rows, hk), BF16),
                        pltpu.VMEM((rows, hk), F32), pltpu.VMEM((rows, hk), F32),
                        pltpu.VMEM((rows, hk), F32), pltpu.VMEM((rows, hk), F32),
                        pltpu.VMEM((rows, hv), F32)],
        compiler_params=_cparams("parallel", "arbitrary"),
        name="gla",
    )(proj, proj, proj, proj, small, jnp.pad(w_up, ((0, LANES - GLA_RANK), (0, 0))), b_gk.reshape(1, hk),
      gain.reshape(1, GLA_DV), tri, ones, seg)


def _head_norm(x, gain):
    return x * lax.rsqrt(jnp.mean(x * x, axis=-1, keepdims=True) + EPS) * gain


def _nsa_prep_kernel(q_ref, ks_ref, vs_ref, kw_ref, vw_ref, ps_ref, qg_ref, kg_ref,
                     q2_ref, ksa_ref, kwp_ref, vst_ref, vwt_ref, gt_ref):
    ts = q_ref.shape[1]
    nblk = LANES - NSA_DK
    zeros = jnp.zeros((ts, NSA_DK), BF16)
    qg = qg_ref[...]
    kg = kg_ref[...]
    for h in range(NSA_HEADS):
        c = slice(h * NSA_DK, (h + 1) * NSA_DK)
        qn = _head_norm(q_ref[0, :, c].astype(F32), qg) * (NSA_DK ** -0.5 * LOG2E)
        q2_ref[0, h // NSA_REP, h % NSA_REP, :, 0:NSA_DK] = qn.astype(BF16)
        q2_ref[0, h // NSA_REP, h % NSA_REP, :, NSA_DK:] = jnp.zeros((ts, q2_ref.shape[4] - NSA_DK), BF16)
    pos = pl.program_id(1) * ts + lax.broadcasted_iota(jnp.int32, (ts, nblk), 0)
    blk = lax.broadcasted_iota(jnp.int32, (ts, nblk), 1)
    onehot = (jnp.right_shift(pos, int(math.log2(SLC_BLOCK))) == blk).astype(BF16)
    gates_t = jax.nn.sigmoid(ps_ref[0]).T
    vs_t = vs_ref[0].astype(F32).T
    vw_t = vw_ref[0].astype(F32).T
    gw = 3 * NSA_REP
    for g in range(NSA_GROUPS):
        c = slice(g * NSA_DK, (g + 1) * NSA_DK)
        ksa_ref[0, g, :, 0:NSA_DK] = _head_norm(ks_ref[0, :, c].astype(F32), kg[1:2, :]).astype(BF16)
        ksa_ref[0, g, :, NSA_DK:] = onehot
        kwp_ref[0, g, :, 0:NSA_DK] = _head_norm(kw_ref[0, :, c].astype(F32), kg[2:3, :]).astype(BF16)
        kwp_ref[0, g, :, NSA_DK:] = zeros
        vst_ref[0, g] = vs_t[c, :].astype(BF16)
        vwt_ref[0, g] = vw_t[c, :].astype(BF16)
        gt_ref[0, g, 0:gw, :] = gates_t[GLA_RANK + g * gw:GLA_RANK + (g + 1) * gw, :]
        gt_ref[0, g, gw:, :] = jnp.zeros((gt_ref.shape[2] - gw, ts), F32)


def _nsa_prep_call(proj, small, q_gain, k_gain, ts=512):
    nb, s, _ = proj.shape
    ts = min(ts, s)
    assert s // SLC_BLOCK <= LANES - NSA_DK
    kvw = NSA_GROUPS * NSA_DK
    qw = NSA_HEADS * NSA_DK
    col = lambda j: (lambda b, i: (b, i, j))
    grp = lambda w: pl.BlockSpec((1, NSA_GROUPS, ts, w), lambda b, i: (b, 0, i, 0))
    grp_t = lambda rows: pl.BlockSpec((1, NSA_GROUPS, rows, ts), lambda b, i: (b, 0, 0, i))
    gate_rows = 16
    return pl.pallas_call(
        _nsa_prep_kernel,
        grid=(nb, s // ts),
        in_specs=[pl.BlockSpec((1, ts, qw), col(2)),
                  pl.BlockSpec((1, ts, kvw), col(14)),
                  pl.BlockSpec((1, ts, kvw), col(15)),
                  pl.BlockSpec((1, ts, kvw), col(16)),
                  pl.BlockSpec((1, ts, kvw), col(17)),
                  pl.BlockSpec((1, ts, LANES), col(0)),
                  pl.BlockSpec((1, NSA_DK), lambda b, i: (0, 0)),
                  pl.BlockSpec((3, NSA_DK), lambda b, i: (0, 0))],
        out_specs=[pl.BlockSpec((1, NSA_GROUPS, NSA_REP, ts, LANES), lambda b, i: (b, 0, 0, i, 0)),
                   grp(LANES), grp(LANES), grp_t(NSA_DK), grp_t(NSA_DK), grp_t(gate_rows)],
        out_shape=[jax.ShapeDtypeStruct((nb, NSA_GROUPS, NSA_REP, s, LANES), BF16),
                   jax.ShapeDtypeStruct((nb, NSA_GROUPS, s, LANES), BF16),
                   jax.ShapeDtypeStruct((nb, NSA_GROUPS, s, LANES), BF16),
                   jax.ShapeDtypeStruct((nb, NSA_GROUPS, NSA_DK, s), BF16),
                   jax.ShapeDtypeStruct((nb, NSA_GROUPS, NSA_DK, s), BF16),
                   jax.ShapeDtypeStruct((nb, NSA_GROUPS, gate_rows, s), F32)],
        compiler_params=_cparams("parallel", "parallel"),
        name="nsa_prep",
    )(proj, proj, proj, proj, proj, small, q_gain.reshape(1, NSA_DK), k_gain)


def _nsa_compress_kernel(hk_ref, hv_ref, pos_ref, w1_ref, w2_ref, kg_ref, kc_ref, vc_ref):
    nh = hk_ref.shape[2]
    half = CMP_STRIDE * NSA_DK
    row = lax.broadcasted_iota(jnp.int32, (nh, 1), 0)
    for kv, (h_ref, o_ref) in enumerate(((hk_ref, kc_ref), (hv_ref, vc_ref))):
        hb = h_ref[0, 0].astype(F32)
        top = _dot((hb + pos_ref[kv, 0:1, :]).astype(BF16), w1_ref[kv, 0:half, :])
        bot = _dot((hb + pos_ref[kv, 1:2, :]).astype(BF16), w1_ref[kv, half:, :])
        hid = _silu(top + pltpu.roll(bot, nh - 1, axis=0))
        out = _dot(hid.astype(BF16), w2_ref[kv])
        if kv == 0:
            out = _head_norm(out, kg_ref[...])
        out = jnp.where(row < nh - 1, out, 0.0)
        if kv == 0:
            o_ref[0, 0, :, 0:NSA_DK] = out.astype(BF16)
            o_ref[0, 0, :, NSA_DK:] = jnp.zeros((nh, LANES - NSA_DK), BF16)
        else:
            o_ref[0, 0] = out.T.astype(BF16)


def _nsa_compress_call(hk, hv, pos, w1, w2, kgain0):
    nb, ng, nh, hw = hk.shape
    hid = w1.shape[2]
    blk = pl.BlockSpec((1, 1, nh, hw), lambda b, g: (b, g, 0, 0))
    out = pl.BlockSpec((1, 1, nh, LANES), lambda b, g: (b, g, 0, 0))
    return pl.pallas_call(
        _nsa_compress_kernel,
        grid=(nb, ng),
        in_specs=[blk, blk,
                  pl.BlockSpec((2, 2, hw), lambda b, g: (0, 0, 0)),
                  pl.BlockSpec((2, 2 * hw, hid), lambda b, g: (0, 0, 0)),
                  pl.BlockSpec((2, hid, NSA_DK), lambda b, g: (0, 0, 0)),
                  pl.BlockSpec((1, NSA_DK), lambda b, g: (0, 0))],
        out_specs=[out, pl.BlockSpec((1, 1, NSA_DK, nh), lambda b, g: (b, g, 0, 0))],
        out_shape=[jax.ShapeDtypeStruct((nb, ng, nh, LANES), BF16),
                   jax.ShapeDtypeStruct((nb, ng, NSA_DK, nh), BF16)],
        compiler_params=_cparams("parallel", "parallel"),
        name="nsa_compress",
    )(hk, hv, pos.reshape(2, 2, hw), w1.astype(BF16), w2.astype(BF16), kgain0.reshape(1, NSA_DK))


def _softmax_cols(s):
    m = jnp.max(s, axis=0, keepdims=True)
    m = jnp.where(m > -jnp.inf, m, 0.0)
    e = jnp.exp2(s - m)
    d = jnp.sum(e, axis=0, keepdims=True)
    return e, 1.0 / jnp.where(d > 0, d, 1.0)


def _query_lanes(slopes_ref, g, q0, tq):
    r = NSA_REP * tq
    lane = lax.broadcasted_iota(jnp.int32, (1, r), 1)
    rep = jnp.zeros((1, r), jnp.int32)
    for i in range(1, NSA_REP):
        rep = rep + (lane >= i * tq).astype(jnp.int32)
    slope = jnp.zeros((1, r), F32)
    for i in range(NSA_REP):
        slope = jnp.where(rep == i, slopes_ref[g * NSA_REP + i], slope)
    return q0 + (lane - rep * tq), slope


def _nsa_select_kernel(slopes_ref, q_ref, kc_ref, vct_ref, mt_ref, psel_ref, qs_ref, oc_ref):
    tq = q_ref.shape[3]
    r = NSA_REP * tq
    nc = kc_ref.shape[2]
    nblk = mt_ref.shape[0]
    g = pl.program_id(1)
    q0 = pl.program_id(2) * tq
    q2 = q_ref[0, 0].reshape(r, LANES)
    t_q, slope = _query_lanes(slopes_ref, g, q0, tq)

    s_c = lax.dot_general(kc_ref[0, 0], q2, _NT, preferred_element_type=F32)
    cend = lax.broadcasted_iota(jnp.int32, (nc, r), 0) * CMP_STRIDE + (CMP_BLOCK - 1)
    dist_c = t_q - cend
    e_c, inv_c = _softmax_cols(jnp.where(dist_c >= 0, s_c - slope * dist_c.astype(F32), -jnp.inf))
    o_c = _dot(vct_ref[0, 0], e_c.astype(BF16)) * inv_c
    for i in range(NSA_REP):
        oc_ref[0, 0, i] = o_c[:, i * tq:(i + 1) * tq].astype(oc_ref.dtype)

    p_sum = e_c[:, 0:tq] * inv_c[:, 0:tq]
    for i in range(1, NSA_REP):
        p_sum = p_sum + e_c[:, i * tq:(i + 1) * tq] * inv_c[:, i * tq:(i + 1) * tq]
    mt = mt_ref[...]
    imp = sum(_dot(mt, piece) for piece in _split3(p_sum))
    blk = lax.broadcasted_iota(jnp.int32, (nblk, 1), 0).astype(F32)
    t_row = q0 + lax.broadcasted_iota(jnp.int32, (1, tq), 1)
    cur = jnp.right_shift(t_row, int(math.log2(SLC_BLOCK))).astype(F32)
    forced = (blk == 0.0) | (blk == cur) | (blk == cur - 1.0)
    valid = blk <= cur
    score = jnp.where(valid, imp + FORCE_BONUS * forced.astype(F32), -jnp.inf)
    sel = jnp.zeros((nblk, tq), F32)
    for _ in range(min(N_SELECT, nblk)):
        best = jnp.max(score, axis=0, keepdims=True)
        first = jnp.min(jnp.where(score == best, blk, float(nblk)), axis=0, keepdims=True)
        pick = blk == first
        sel = jnp.where(pick, 1.0, sel)
        score = jnp.where(pick, -jnp.inf, score)
    sel_bias = jnp.where((sel > 0.0) & valid, 0.0, -MASK_BIG).astype(BF16)
    bias_q = lax.dot_general(sel_bias, psel_ref[...], _TN, preferred_element_type=F32)
    bias_q = bias_q.astype(BF16)
    for i in range(NSA_REP):
        qs_ref[0, 0, i] = q_ref[0, 0, i] + bias_q


def _nsa_attn_kernel(slopes_ref, q_ref, qs_ref, oc_ref, ksa_ref, kwp_ref, vst_ref, vwt_ref, gt_ref, wb_ref,
                     o_ref, sa0_ref, sa1_ref, sp0_ref, sp1_ref):
    tq = q_ref.shape[3]
    r = NSA_REP * tq
    seq = ksa_ref.shape[2]
    g = pl.program_id(1)
    q0 = pl.program_id(2) * tq
    q2 = q_ref[0, 0].reshape(r, LANES)
    q_sel = qs_ref[0, 0].reshape(r, LANES)
    t_q, slope = _query_lanes(slopes_ref, g, q0, tq)

    tks = min(NSA_TKS, seq)
    pos0 = lax.broadcasted_iota(jnp.int32, (tks, r), 0)
    sb = slope * pos0.astype(F32)

    def sel_scores(j):
        k0 = pl.multiple_of(j * tks, tks)
        return lax.dot_general(ksa_ref[0, 0, pl.ds(k0, tks), :], q_sel, _NT, preferred_element_type=F32) + sb

    def sel_values(j, p):
        return _dot(vst_ref[0, 0, :, pl.ds(pl.multiple_of(j * tks, tks), tks)], p)

    def sel_softmax(j, a, m, l):
        c = slope * (j * tks + jnp.zeros((1, r), jnp.int32)).astype(F32)
        m_new = jnp.maximum(m, jnp.max(a, axis=0, keepdims=True) + c)
        m_safe = jnp.where(m_new > -jnp.inf, m_new, 0.0)
        p = jnp.exp2(a - (m_safe - c))
        alpha = jnp.exp2(m - m_safe)
        return p.astype(BF16), alpha, m_new, alpha * l + jnp.sum(p, axis=0, keepdims=True)

    def sel_step(j, a_cur, a_next, p_prev, p_cur, carry):
        m, l, acc = carry
        a_next[...] = sel_scores(j + 1)
        acc = acc + sel_values(jnp.maximum(j - 1, 0), p_prev[...])
        p, alpha, m, l = sel_softmax(j, a_cur[...], m, l)
        p_cur[...] = p
        return m, l, alpha * acc

    def sel_pair(i, carry):
        carry = sel_step(2 * i, sa0_ref, sa1_ref, sp1_ref, sp0_ref, carry)
        return sel_step(2 * i + 1, sa1_ref, sa0_ref, sp0_ref, sp1_ref, carry)

    def sel_finish(a_ref, p_ref, carry):
        m, l, acc = carry
        acc = acc + sel_values(jnp.maximum(n_big - 1, 0), p_ref[...])
        a_d = jnp.where(n_big * tks + pos0 <= t_q, a_ref[...], -jnp.inf)
        p_d, alpha_d, _, l = sel_softmax(n_big, a_d, m, l)
        acc = alpha_d * acc + sel_values(n_big, p_d)
        return acc * (1.0 / jnp.where(l > 0, l, 1.0))

    n_big = q0 // tks
    sa0_ref[...] = sel_scores(0)
    sp1_ref[...] = jnp.zeros((tks, r), BF16)
    carry = (jnp.full((1, r), -jnp.inf, F32), jnp.zeros((1, r), F32), jnp.zeros((NSA_DK, r), F32))
    carry = lax.fori_loop(0, n_big // 2, sel_pair, carry)
    o_s = lax.cond(n_big % 2 == 1,
                   lambda cr: sel_finish(sa1_ref, sp0_ref, sel_step(n_big - 1, sa0_ref, sa1_ref, sp1_ref, sp0_ref, cr)),
                   lambda cr: sel_finish(sa0_ref, sp1_ref, cr), carry)

    span = WINDOW + tq

    def window(start, bias_fn):
        s_w = lax.dot_general(kwp_ref[0, 0, pl.ds(start, span), :], q2, _NT, preferred_element_type=F32)
        e_w, inv_w = _softmax_cols(bias_fn(s_w))
        return _dot(vwt_ref[0, 0, :, pl.ds(start, span)], e_w.astype(BF16)) * inv_w

    def window_head(s_w):
        dist_w = t_q - lax.broadcasted_iota(jnp.int32, (span, r), 0)
        return jnp.where((dist_w >= 0) & (dist_w < WINDOW), s_w - slope * dist_w.astype(F32), -jnp.inf)

    o_w = lax.cond(q0 >= WINDOW,
                   lambda: window(pl.multiple_of(q0 - WINDOW, tq), lambda s_w: s_w + wb_ref[0]),
                   lambda: window(0, window_head))

    gt = gt_ref[0, 0]
    outs = []
    for i in range(NSA_REP):
        ls = slice(i * tq, (i + 1) * tq)
        outs.append(gt[3 * i:3 * i + 1, :] * oc_ref[0, 0, i].astype(F32) + gt[3 * i + 1:3 * i + 2, :] * o_s[:, ls]
                    + gt[3 * i + 2:3 * i + 3, :] * o_w[:, ls])
    o_ref[0] = jnp.concatenate(outs, axis=0).T.astype(o_ref.dtype)


def _alibi_slopes(n):
    return np.asarray([2.0 ** (-8.0 * (i + 1) / n) for i in range(n)], dtype=np.float32)


def _cmp_to_slc_t(s, nc_pad):
    n_cmp = (s - CMP_BLOCK) // CMP_STRIDE + 1
    cs = np.arange(n_cmp) * CMP_STRIDE
    ss = np.arange(s // SLC_BLOCK) * SLC_BLOCK
    ov = np.clip(np.minimum(cs[:, None] + CMP_BLOCK, ss[None, :] + SLC_BLOCK)
                 - np.maximum(cs[:, None], ss[None, :]), 0, None)
    m = np.zeros((nc_pad, s // SLC_BLOCK), np.float32)
    m[:n_cmp] = ov / CMP_STRIDE
    return m.T


def _nsa_select_call(q2, kc, vct):
    nb, ng, nrep, s, _ = q2.shape
    tq = min(NSA_TSEL, s)
    nc = kc.shape[2]
    nblk = s // SLC_BLOCK
    mt = jnp.asarray(_cmp_to_slc_t(s, nc), BF16)
    psel = np.zeros((nblk, LANES), np.float32)
    psel[np.arange(nblk), NSA_DK + np.arange(nblk)] = 1.0
    slopes = _alibi_slopes(NSA_HEADS).astype(np.float64) * LOG2E
    q_spec = pl.BlockSpec((1, 1, nrep, tq, LANES), lambda b, g, i: (b, g, 0, i, 0))
    return pl.pallas_call(
        _nsa_select_kernel,
        grid=(nb, ng, s // tq),
        in_specs=[pl.BlockSpec(memory_space=pltpu.SMEM), q_spec,
                  pl.BlockSpec((1, 1, nc, LANES), lambda b, g, i: (b, g, 0, 0)),
                  pl.BlockSpec((1, 1, NSA_DK, nc), lambda b, g, i: (b, g, 0, 0)),
                  pl.BlockSpec((nblk, nc), lambda b, g, i: (0, 0)),
                  pl.BlockSpec((nblk, LANES), lambda b, g, i: (0, 0))],
        out_specs=[q_spec, pl.BlockSpec((1, 1, nrep, NSA_DK, tq), lambda b, g, i: (b, g, 0, 0, i))],
        out_shape=[jax.ShapeDtypeStruct(q2.shape, BF16),
                   jax.ShapeDtypeStruct((nb, ng, nrep, NSA_DK, s), BF16)],
        compiler_params=_cparams("parallel", "parallel", "arbitrary"),
        name="nsa_select",
    )(jnp.asarray(slopes, F32), q2, kc, vct, mt, jnp.asarray(psel, BF16))


def _nsa_attn_call(q2, q_sel, oc, ksa, kwp, vst, vwt, gt):
    nb, ng, nrep, s, _ = q2.shape
    tq = min(NSA_TQ, s)
    tks = min(NSA_TKS, s)
    assert s >= WINDOW + tq and WINDOW % tq == 0
    slopes = _alibi_slopes(NSA_HEADS).astype(np.float64) * LOG2E
    dist = WINDOW + np.arange(tq)[None, :] - np.arange(WINDOW + tq)[:, None]
    wbias = np.where((dist >= 0) & (dist < WINDOW), -slopes[:, None, None] * dist[None], -np.inf)
    wbias = wbias.reshape(ng, nrep, WINDOW + tq, tq).transpose(0, 2, 1, 3).reshape(ng, WINDOW + tq, nrep * tq)
    seq = pl.BlockSpec((1, 1, s, LANES), lambda b, g, i: (b, g, 0, 0))
    seq_t = pl.BlockSpec((1, 1, NSA_DK, s), lambda b, g, i: (b, g, 0, 0))
    q_spec = pl.BlockSpec((1, 1, nrep, tq, LANES), lambda b, g, i: (b, g, 0, i, 0))
    return pl.pallas_call(
        _nsa_attn_kernel,
        grid=(nb, ng, s // tq),
        in_specs=[pl.BlockSpec(memory_space=pltpu.SMEM), q_spec, q_spec,
                  pl.BlockSpec((1, 1, nrep, NSA_DK, tq), lambda b, g, i: (b, g, 0, 0, i)),
                  seq, seq, seq_t, seq_t,
                  pl.BlockSpec((1, 1, gt.shape[2], tq), lambda b, g, i: (b, g, 0, i)),
                  pl.BlockSpec((1, WINDOW + tq, nrep * tq), lambda b, g, i: (g, 0, 0))],
        out_specs=pl.BlockSpec((1, tq, nrep * NSA_DK), lambda b, g, i: (b, i, g)),
        out_shape=jax.ShapeDtypeStruct((nb, s, ng * nrep * NSA_DK), BF16),
        scratch_shapes=[pltpu.VMEM((tks, nrep * tq), F32), pltpu.VMEM((tks, nrep * tq), F32),
                        pltpu.VMEM((tks, nrep * tq), BF16), pltpu.VMEM((tks, nrep * tq), BF16)],
        compiler_params=_cparams("parallel", "parallel", "arbitrary"),
        name="nsa_attention",
    )(jnp.asarray(slopes, F32), q2, q_sel, oc, ksa, kwp, vst, vwt, gt, jnp.asarray(wbias, F32))


def _out_proj_kernel(a0_ref, a1_ref, a2_ref, w0_ref, w1_ref, w2_ref, x_ref, g_ref, o_ref):
    acc = _dot(a0_ref[...], w0_ref[...]) + _dot(a1_ref[...], w1_ref[...]) + _dot(a2_ref[...], w2_ref[...])
    o_ref[...] = x_ref[...] + g_ref[0] * acc


def _out_proj_call(pool_o, gla_o, nsa_o, w_out, x2d, gate, seq, tm=1024, tn=1024):
    m, d = x2d.shape
    tm = min(tm, seq)
    k0, k1, k2 = pool_o.shape[1], gla_o.shape[1], nsa_o.shape[1]
    assert k0 == k1 and k2 == 2 * k0
    per_b = seq // tm
    a = lambda k: pl.BlockSpec((tm, k), lambda j, i: (i, 0))
    return pl.pallas_call(
        _out_proj_kernel,
        grid=(d // tn, m // tm),
        in_specs=[a(k0), a(k1), a(k2),
                  pl.BlockSpec((k0, tn), lambda j, i: (0, j)),
                  pl.BlockSpec((k1, tn), lambda j, i: (1, j)),
                  pl.BlockSpec((k2, tn), lambda j, i: (1, j)),
                  pl.BlockSpec((tm, tn), lambda j, i: (i, j)),
                  pl.BlockSpec((1, 1, tn), lambda j, i: (i // per_b, 0, j))],
        out_specs=pl.BlockSpec((tm, tn), lambda j, i: (i, j)),
        out_shape=jax.ShapeDtypeStruct((m, d), F32),
        compiler_params=_cparams("parallel", "parallel"),
        name="out_proj_residual",
    )(pool_o, gla_o, nsa_o, w_out, w_out, w_out, x2d, gate.reshape(-1, 1, d))


def _ffn_kernel(x_ref, gn_ref, sh_ref, sc_ref, wg_ref, wu_ref, wo_ref, g_ref, o_ref, acc_ref, h_ref):
    j = pl.program_id(1)

    @pl.when(j == 0)
    def _():
        acc_ref[...] = jnp.zeros_like(acc_ref)
        h_ref[...] = _norm_mod(x_ref[...], gn_ref[...], sh_ref[0], sc_ref[0])

    h = h_ref[...]
    act = (_silu(_dot(h, wg_ref[...])) * _dot(h, wu_ref[...])).astype(BF16)
    acc_ref[...] += _dot(act, wo_ref[...])

    @pl.when(j == pl.num_programs(1) - 1)
    def _():
        o_ref[...] = x_ref[...] + g_ref[0] * acc_ref[...]


def _ffn_call(x2d, gain, shift, scale, w_in, w_out, gate, seq, tm=512, th=512):
    m, d = x2d.shape
    tm = min(tm, seq)
    hid = w_out.shape[0]
    nh = hid // th
    per_b = seq // tm
    per_batch = pl.BlockSpec((1, 1, d), lambda i, j: (i // per_b, 0, 0))
    return pl.pallas_call(
        _ffn_kernel,
        grid=(m // tm, nh),
        in_specs=[pl.BlockSpec((tm, d), lambda i, j: (i, 0)),
                  pl.BlockSpec((1, d), lambda i, j: (0, 0)), per_batch, per_batch,
                  pl.BlockSpec((d, th), lambda i, j: (0, j)),
                  pl.BlockSpec((d, th), lambda i, j: (0, j + nh)),
                  pl.BlockSpec((th, d), lambda i, j: (j, 0)),
                  per_batch],
        out_specs=pl.BlockSpec((tm, d), lambda i, j: (i, 0)),
        out_shape=jax.ShapeDtypeStruct((m, d), F32),
        scratch_shapes=[pltpu.VMEM((tm, d), F32), pltpu.VMEM((tm, d), BF16)],
        compiler_params=_cparams("parallel", "arbitrary"),
        name="ffn_swiglu_residual",
    )(x2d, gain.reshape(1, d), shift.reshape(-1, 1, d), scale.reshape(-1, 1, d), w_in, w_in, w_out,
      gate.reshape(-1, 1, d))


def _regroup_w_in(w):
    d = w.shape[0]
    a = 4 * POOL_GROUP + 2 * GLA_HEADS * GLA_DK + 2 * GLA_HEADS * GLA_DV
    b = a + GLA_RANK
    c = b + NSA_HEADS * NSA_DK + 6 * NSA_GROUPS * NSA_DK
    ng = 3 * NSA_HEADS
    main = jnp.concatenate([w[:, :a], w[:, b:c]], axis=1).astype(BF16)
    small = jnp.concatenate([w[:, a:b], w[:, c:c + ng], jnp.zeros((d, LANES - GLA_RANK - ng), w.dtype)],
                            axis=1).astype(BF16)
    return main, small


def _layer(x, mod, p):
    nb, s, d = x.shape
    sh_a, sc_a, g_a, sh_f, sc_f, g_f = [mod[:, i * d:(i + 1) * d] for i in range(6)]

    x2d = x.reshape(nb * s, d)
    w_main, w_small = _regroup_w_in(p["w_in"])
    proj, small = _in_proj_call(x2d, p["norm_mix_gain"], sh_a, sc_a, w_main, w_small, s)
    proj = proj.reshape(nb, s, -1)
    small = small[0].reshape(nb, s, LANES)

    pool_o = _pool_call(proj, p["pool_w"], p["pool_scale"])
    gla_o = _gla_call(proj, small, p["gla_w_gk_up"], p["gla_b_gk"], p["gla_norm_gain"])

    q2, ksa, kwp, vst, vwt, gt = _nsa_prep_call(proj, small, p["nsa_q_gain"], p["nsa_k_gain"])
    kv0 = 4 * POOL_GROUP + 2 * GLA_HEADS * (GLA_DK + GLA_DV) + NSA_HEADS * NSA_DK
    kvw = NSA_GROUPS * NSA_DK

    def half_blocks(cols):
        t = cols.reshape(nb, s // CMP_STRIDE, CMP_STRIDE, NSA_GROUPS, NSA_DK)
        return jnp.transpose(t, (0, 3, 1, 2, 4)).reshape(nb, NSA_GROUPS, s // CMP_STRIDE, CMP_STRIDE * NSA_DK)

    kc, vc = _nsa_compress_call(half_blocks(proj[:, :, kv0:kv0 + kvw]),
                                half_blocks(proj[:, :, kv0 + kvw:kv0 + 2 * kvw]),
                                p["nsa_cmp_pos"], p["nsa_cmp_w1"], p["nsa_cmp_w2"], p["nsa_k_gain"][0])
    q_sel, oc = _nsa_select_call(q2, kc, vc)
    nsa_o = _nsa_attn_call(q2, q_sel, oc, ksa, kwp, vst, vwt, gt)

    x1 = _out_proj_call(pool_o.reshape(nb * s, -1), gla_o.reshape(nb * s, -1), nsa_o.reshape(nb * s, -1),
                        p["w_out"].astype(BF16), x2d, g_a, s)
    x2 = _ffn_call(x1, p["norm_ffn_gain"], sh_f, sc_f, p["w_ffn_in"].astype(BF16), p["w_ffn_out"].astype(BF16),
                   g_f, s)
    return x2.reshape(nb, s, d)


_PER_LAYER = ("norm_mix_gain", "norm_ffn_gain", "w_in", "w_out", "pool_w", "pool_scale", "gla_w_gk_up",
              "gla_b_gk", "gla_norm_gain", "nsa_q_gain", "nsa_k_gain", "nsa_cmp_pos", "nsa_cmp_w1",
              "nsa_cmp_w2", "w_ffn_in", "w_ffn_out")


def kernel(x, c, norm_mix_gain, norm_ffn_gain, w_mod, b_mod, w_in, w_out, pool_w, pool_scale, gla_w_gk_up,
           gla_b_gk, gla_norm_gain, nsa_q_gain, nsa_k_gain, nsa_cmp_pos, nsa_cmp_w1, nsa_cmp_w2, w_ffn_in,
           w_ffn_out):
    params = dict(norm_mix_gain=norm_mix_gain, norm_ffn_gain=norm_ffn_gain, w_in=w_in, w_out=w_out,
                  pool_w=pool_w, pool_scale=pool_scale, gla_w_gk_up=gla_w_gk_up, gla_b_gk=gla_b_gk,
                  gla_norm_gain=gla_norm_gain, nsa_q_gain=nsa_q_gain, nsa_k_gain=nsa_k_gain,
                  nsa_cmp_pos=nsa_cmp_pos, nsa_cmp_w1=nsa_cmp_w1, nsa_cmp_w2=nsa_cmp_w2,
                  w_ffn_in=w_ffn_in, w_ffn_out=w_ffn_out)
    mod = _mod_call(c, w_mod, b_mod)
    for l in range(w_mod.shape[0]):
        x = _layer(x, mod[l], {k: params[k][l] for k in _PER_LAYER})
    return x
```

```python
import functools
import math

import jax
import jax.numpy as jnp
import numpy as np
from jax import lax
from jax.experimental import pallas as pl
from jax.experimental.pallas import tpu as pltpu

F32 = jnp.float32
BF16 = jnp.bfloat16

POOL_WINDOWS = (2, 4, 8, 16)
POOL_GROUP = 128
GLA_HEADS = 4
GLA_DK = 64
GLA_DV = 128
GLA_RANK = 16
GLA_GATE_NORM = 16.0
NSA_HEADS = 16
NSA_GROUPS = 4
NSA_REP = NSA_HEADS // NSA_GROUPS
NSA_DK = 64
CMP_BLOCK = 32
CMP_STRIDE = 16
SLC_BLOCK = 64
N_SELECT = 16
WINDOW = 512
FORCE_BONUS = 100.0
EPS = 1e-6

LANES = 128
VMEM_LIMIT = 56 * 1024 * 1024

GLA_SUB = 16
GLA_ROWS = 256
NSA_TSEL = 512
NSA_TQ = 256
NSA_TKS = 512
NSA_TKW = 128
MASK_BIG = 2.0 ** 100
LOG2E = math.log2(math.e)

_NT = (((1,), (1,)), ((), ()))
_TN = (((0,), (0,)), ((), ()))


def _cparams(*sem):
    return pltpu.CompilerParams(dimension_semantics=sem, vmem_limit_bytes=VMEM_LIMIT)


def _dot(a, b):
    return jnp.dot(a, b, preferred_element_type=F32)


def _split3(x):
    hi = x.astype(BF16)
    r1 = x - hi.astype(F32)
    mid = r1.astype(BF16)
    lo = (r1 - mid.astype(F32)).astype(BF16)
    return hi, mid, lo


def _silu(x):
    return x * jax.nn.sigmoid(x)


def _mod_kernel(ct_ref, w_ref, b_ref, o_ref):
    ct = ct_ref[...]
    act = _silu(ct)
    w = w_ref[0]
    for b in range(ct.shape[1]):
        row = jnp.sum(act[:, b:b + 1] * w, axis=0, keepdims=True)
        o_ref[0, b:b + 1, :] = row + b_ref[0]


def _mod_call(c, w_mod, b_mod, tn=512):
    nl, d, n = w_mod.shape
    nb = c.shape[0]
    return pl.pallas_call(
        _mod_kernel,
        grid=(nl, n // tn),
        in_specs=[pl.BlockSpec((d, nb), lambda l, j: (0, 0)),
                  pl.BlockSpec((1, d, tn), lambda l, j: (l, 0, j)),
                  pl.BlockSpec((1, 1, tn), lambda l, j: (l, 0, j))],
        out_specs=pl.BlockSpec((1, nb, tn), lambda l, j: (l, 0, j)),
        out_shape=jax.ShapeDtypeStruct((nl, nb, n), F32),
        compiler_params=_cparams("parallel", "parallel"),
        name="adaln_mod",
    )(c.T, w_mod, b_mod.reshape(nl, 1, n))


def _norm_mod(x, gain, shift, scale):
    y = x * lax.rsqrt(jnp.mean(x * x, axis=-1, keepdims=True) + EPS) * gain
    return (y * (1.0 + scale) + shift).astype(BF16)


def _in_proj_kernel(x_ref, gn_ref, sh_ref, sc_ref, w_ref, ws_ref, o_ref, os_ref):
    h = _norm_mod(x_ref[...], gn_ref[...], sh_ref[0], sc_ref[0])
    o_ref[...] = _dot(h, w_ref[...]).astype(o_ref.dtype)
    os_ref[0] = _dot(h, ws_ref[...])


def _in_proj_call(x2d, gain, shift, scale, w_main, w_small, seq, tm=1024, tn=1536):
    m, d = x2d.shape
    tm = min(tm, seq)
    n = w_main.shape[1]
    per_b = seq // tm
    per_batch = pl.BlockSpec((1, 1, d), lambda j, i: (i // per_b, 0, 0))
    return pl.pallas_call(
        _in_proj_kernel,
        grid=(n // tn, m // tm),
        in_specs=[pl.BlockSpec((tm, d), lambda j, i: (i, 0)),
                  pl.BlockSpec((1, d), lambda j, i: (0, 0)), per_batch, per_batch,
                  pl.BlockSpec((d, tn), lambda j, i: (0, j)),
                  pl.BlockSpec((d, LANES), lambda j, i: (0, 0))],
        out_specs=[pl.BlockSpec((tm, tn), lambda j, i: (i, j)),
                   pl.BlockSpec((1, tm, LANES), lambda j, i: (j, i, 0))],
        out_shape=[jax.ShapeDtypeStruct((m, n), BF16), jax.ShapeDtypeStruct((n // tn, m, LANES), F32)],
        compiler_params=_cparams("parallel", "parallel"),
        name="norm_in_proj",
    )(x2d, gain.reshape(1, d), shift.reshape(-1, 1, d), scale.reshape(-1, 1, d), w_main, w_small)


def _pool_kernel(u_ref, w_ref, sc_ref, o_ref):
    s = u_ref.shape[1]
    t = lax.broadcasted_iota(jnp.int32, (s, 1), 0)
    for gi, win in enumerate(POOL_WINDOWS):
        cols = slice(gi * POOL_GROUP, (gi + 1) * POOL_GROUP)
        u = u_ref[0, :, cols].astype(F32)
        acc = u
        k = 1
        while k < win:
            acc = acc + jnp.where(t >= k, pltpu.roll(acc, k, axis=0), 0.0)
            k *= 2
        cnt = jnp.minimum(t + 1, win).astype(F32)
        dlt = (acc / cnt - u).astype(BF16)
        y = _dot(dlt, w_ref[gi])
        o_ref[0, :, cols] = (y * sc_ref[:, cols]).astype(o_ref.dtype)


def _pool_call(proj, pool_w, pool_scale):
    nb, s, _ = proj.shape
    pw = len(POOL_WINDOWS) * POOL_GROUP
    return pl.pallas_call(
        _pool_kernel,
        grid=(nb,),
        in_specs=[pl.BlockSpec((1, s, pw), lambda b: (b, 0, 0)),
                  pl.BlockSpec(pool_w.shape, lambda b: (0, 0, 0)),
                  pl.BlockSpec((1, pw), lambda b: (0, 0))],
        out_specs=pl.BlockSpec((1, s, pw), lambda b: (b, 0, 0)),
        out_shape=jax.ShapeDtypeStruct((nb, s, pw), BF16),
        compiler_params=_cparams("parallel"),
        name="pool_mixer",
    )(proj, pool_w.astype(BF16), pool_scale.reshape(1, pw))


def _gla_kernel(q_ref, k_ref, v_ref, g_ref, ps_ref, wup_ref, bgk_ref, gain_ref,
                tri_ref, ones_ref, seg_ref, o_ref,
                st_ref, qin_s, kdec_s, b_s, dl_s, qf_s, kf_s, o_s):
    rows = q_ref.shape[1]
    hk = GLA_HEADS * GLA_DK

    @pl.when(pl.program_id(1) == 0)
    def _():
        st_ref[...] = jnp.zeros_like(st_ref)

    low_hi, low_mid, _ = _split3(ps_ref[0])
    wup_hi, wup_mid, _ = _split3(wup_ref[...])
    x = bgk_ref[...] + (_dot(low_hi, wup_hi) + _dot(low_hi, wup_mid) + _dot(low_mid, wup_hi))
    la =(jnp.minimum(x, 0.0) - jnp.log1p(jnp.exp(-jnp.abs(x)))) * (1.0 / GLA_GATE_NORM)
    pieces = _split3(la)
    tri = tri_ref[...]
    ones = ones_ref[...]
    bcum = _dot(tri, pieces[0]) + _dot(tri, pieces[1]) + _dot(tri, pieces[2])
    tot = _dot(ones, pieces[0]) + _dot(ones, pieces[1]) + _dot(ones, pieces[2])
    q = q_ref[0].astype(F32) * (GLA_DK ** -0.5)
    k = k_ref[0].astype(F32)
    qf_s[...] = q
    kf_s[...] = k
    b_s[...] = bcum
    qin_s[...] = (q * jnp.exp(bcum)).astype(BF16)
    kdec_s[...] = (k * jnp.exp(tot - bcum)).astype(BF16)
    dl_s[...] = jnp.exp(tot)

    sub_i = lax.broadcasted_iota(jnp.int32, (GLA_SUB, 1), 0)
    seg = seg_ref[...]

    def step(n, carry):
        r0 = pl.multiple_of(n * GLA_SUB, GLA_SUB)
        rs = pl.ds(r0, GLA_SUB)
        bn = b_s[rs, :]
        qn = qf_s[rs, :]
        kn = kf_s[rs, :]
        vn = v_ref[0, rs, :]
        vf = vn.astype(F32)
        ws = []
        for j in range(GLA_SUB):
            e = jnp.exp(jnp.where(sub_i >= j, bn - bn[j:j + 1, :], -jnp.inf))
            ws.append((qn * (kn[j:j + 1, :] * e)).astype(BF16))
        z = _dot(jnp.concatenate(ws, axis=0), seg)
        intra = z[0:GLA_SUB, :] * vf[0:1, :]
        for j in range(1, GLA_SUB):
            intra = intra + z[j * GLA_SUB:(j + 1) * GLA_SUB, :] * vf[j:j + 1, :]
        qin = qin_s[rs, :]
        kdec = kdec_s[rs, :]
        dl = dl_s[pl.ds(r0, 1), :]
        st = st_ref[...]
        stb = st.astype(BF16)
        st_new = st * dl
        for h in range(GLA_HEADS):
            ck = slice(h * GLA_DK, (h + 1) * GLA_DK)
            cv = slice(h * GLA_DV, (h + 1) * GLA_DV)
            inter = lax.dot_general(qin[:, ck], stb[:, ck], _NT, preferred_element_type=F32)
            o_s[rs, cv] = inter + intra[:, cv]
            upd = lax.dot_general(vn[:, cv], kdec[:, ck], _TN, preferred_element_type=F32)
            st_ref[:, ck] = st_new[:, ck] + upd
        return carry

    lax.fori_loop(0, rows // GLA_SUB, step, 0)

    gain = gain_ref[...]
    for h in range(GLA_HEADS):
        cv = slice(h * GLA_DV, (h + 1) * GLA_DV)
        o = o_s[:, cv]
        y = o * lax.rsqrt(jnp.mean(o * o, axis=-1, keepdims=True) + EPS) * gain
        o_ref[0, :, cv] = (y * _silu(g_ref[0, :, cv].astype(F32))).astype(o_ref.dtype)


def _gla_call(proj, small, w_up, b_gk, gain):
    nb, s, _ = proj.shape
    rows = min(GLA_ROWS, s)
    hk, hv = GLA_HEADS * GLA_DK, GLA_HEADS * GLA_DV
    idx = np.arange(rows)
    same = (idx[:, None] // GLA_SUB) == (idx[None, :] // GLA_SUB)
    tri = jnp.asarray(same & (idx[None, :] <= idx[:, None]), BF16)
    ones = jnp.asarray(same, BF16)
    seg = jnp.asarray((np.arange(hk)[:, None] // GLA_DK) == (np.arange(hv)[None, :] // GLA_DV), BF16)
    const = lambda shape: pl.BlockSpec(shape, lambda b, i: (0,) * len(shape))
    return pl.pallas_call(
        _gla_kernel,
        grid=(nb, s // rows),
        in_specs=[pl.BlockSpec((1, rows, hk), lambda b, i: (b, i, 2)),
                  pl.BlockSpec((1, rows, hk), lambda b, i: (b, i, 3)),
                  pl.BlockSpec((1, rows, hv), lambda b, i: (b, i, 2)),
                  pl.BlockSpec((1, rows, hv), lambda b, i: (b, i, 3)),
                  pl.BlockSpec((1, rows, LANES), lambda b, i: (b, i, 0)),
                  const((LANES, hk)), const((1, hk)), const((1, GLA_DV)),
                  const((rows, rows)), const((rows, rows)), const((hk, hv))],
        out_specs=pl.BlockSpec((1, rows, hv), lambda b, i: (b, i, 0)),
        out_shape=jax.ShapeDtypeStruct((nb, s, hv), BF16),
        scratch_shapes=[pltpu.VMEM((GLA_DV, hk), F32),
                        pltpu.VMEM((rows, hk), BF16), pltpu.VMEM((rows, hk), BF16),
                        pltpu.VMEM((rows, hk), F32), pltpu.VMEM((rows, hk), F32),
                        pltpu.VMEM((rows, hk), F32), pltpu.VMEM((rows, hk), F32),
                        pltpu.VMEM((rows, hv), F32)],
        compiler_params=_cparams("parallel", "arbitrary"),
        name="gla",
    )(proj, proj, proj, proj, small, jnp.pad(w_up, ((0, LANES - GLA_RANK), (0, 0))), b_gk.reshape(1, hk),
      gain.reshape(1, GLA_DV), tri, ones, seg)


def _head_norm(x, gain):
    return x * lax.rsqrt(jnp.mean(x * x, axis=-1, keepdims=True) + EPS) * gain


def _nsa_prep_kernel(q_ref, ks_ref, vs_ref, kw_ref, vw_ref, ps_ref, qg_ref, kg_ref,
                     q2_ref, ksa_ref, kwp_ref, vst_ref, vwt_ref, gt_ref):
    ts = q_ref.shape[1]
    nblk = LANES - NSA_DK
    zeros = jnp.zeros((ts, NSA_DK), BF16)
    qg = qg_ref[...]
    kg = kg_ref[...]
    for h in range(NSA_HEADS):
        c = slice(h * NSA_DK, (h + 1) * NSA_DK)
        qn = _head_norm(q_ref[0, :, c].astype(F32), qg) * (NSA_DK ** -0.5 * LOG2E)
        q2_ref[0, h // NSA_REP, h % NSA_REP, :, 0:NSA_DK] = qn.astype(BF16)
        q2_ref[0, h // NSA_REP, h % NSA_REP, :, NSA_DK:] = jnp.zeros((ts, q2_ref.shape[4] - NSA_DK), BF16)
    pos = pl.program_id(1) * ts + lax.broadcasted_iota(jnp.int32, (ts, nblk), 0)
    blk = lax.broadcasted_iota(jnp.int32, (ts, nblk), 1)
    onehot = (jnp.right_shift(pos, int(math.log2(SLC_BLOCK))) == blk).astype(BF16)
    gates_t = jax.nn.sigmoid(ps_ref[0]).T
    vs_t = vs_ref[0].astype(F32).T
    vw_t = vw_ref[0].astype(F32).T
    gw = 3 * NSA_REP
    for g in range(NSA_GROUPS):
        c = slice(g * NSA_DK, (g + 1) * NSA_DK)
        ksa_ref[0, g, :, 0:NSA_DK] = _head_norm(ks_ref[0, :, c].astype(F32), kg[1:2, :]).astype(BF16)
        ksa_ref[0, g, :, NSA_DK:] = onehot
        kwp_ref[0, g, :, 0:NSA_DK] = _head_norm(kw_ref[0, :, c].astype(F32), kg[2:3, :]).astype(BF16)
        kwp_ref[0, g, :, NSA_DK:] = zeros
        vst_ref[0, g] = vs_t[c, :].astype(BF16)
        vwt_ref[0, g] = vw_t[c, :].astype(BF16)
        gt_ref[0, g, 0:gw, :] = gates_t[GLA_RANK + g * gw:GLA_RANK + (g + 1) * gw, :]
        gt_ref[0, g, gw:, :] = jnp.zeros((gt_ref.shape[2] - gw, ts), F32)


def _nsa_prep_call(proj, small, q_gain, k_gain, ts=512):
    nb, s, _ = proj.shape
    ts = min(ts, s)
    assert s // SLC_BLOCK <= LANES - NSA_DK
    kvw = NSA_GROUPS * NSA_DK
    qw = NSA_HEADS * NSA_DK
    col = lambda j: (lambda b, i: (b, i, j))
    grp = lambda w: pl.BlockSpec((1, NSA_GROUPS, ts, w), lambda b, i: (b, 0, i, 0))
    grp_t = lambda rows: pl.BlockSpec((1, NSA_GROUPS, rows, ts), lambda b, i: (b, 0, 0, i))
    gate_rows = 16
    return pl.pallas_call(
        _nsa_prep_kernel,
        grid=(nb, s // ts),
        in_specs=[pl.BlockSpec((1, ts, qw), col(2)),
                  pl.BlockSpec((1, ts, kvw), col(14)),
                  pl.BlockSpec((1, ts, kvw), col(15)),
                  pl.BlockSpec((1, ts, kvw), col(16)),
                  pl.BlockSpec((1, ts, kvw), col(17)),
                  pl.BlockSpec((1, ts, LANES), col(0)),
                  pl.BlockSpec((1, NSA_DK), lambda b, i: (0, 0)),
                  pl.BlockSpec((3, NSA_DK), lambda b, i: (0, 0))],
        out_specs=[pl.BlockSpec((1, NSA_GROUPS, NSA_REP, ts, LANES), lambda b, i: (b, 0, 0, i, 0)),
                   grp(LANES), grp(LANES), grp_t(NSA_DK), grp_t(NSA_DK), grp_t(gate_rows)],
        out_shape=[jax.ShapeDtypeStruct((nb, NSA_GROUPS, NSA_REP, s, LANES), BF16),
                   jax.ShapeDtypeStruct((nb, NSA_GROUPS, s, LANES), BF16),
                   jax.ShapeDtypeStruct((nb, NSA_GROUPS, s, LANES), BF16),
                   jax.ShapeDtypeStruct((nb, NSA_GROUPS, NSA_DK, s), BF16),
                   jax.ShapeDtypeStruct((nb, NSA_GROUPS, NSA_DK, s), BF16),
                   jax.ShapeDtypeStruct((nb, NSA_GROUPS, gate_rows, s), F32)],
        compiler_params=_cparams("parallel", "parallel"),
        name="nsa_prep",
    )(proj, proj, proj, proj, proj, small, q_gain.reshape(1, NSA_DK), k_gain)


def _nsa_compress_kernel(hk_ref, hv_ref, pos_ref, w1_ref, w2_ref, kg_ref, kc_ref, vc_ref):
    nh = hk_ref.shape[2]
    half = CMP_STRIDE * NSA_DK
    row = lax.broadcasted_iota(jnp.int32, (nh, 1), 0)
    for kv, (h_ref, o_ref) in enumerate(((hk_ref, kc_ref), (hv_ref, vc_ref))):
        hb = h_ref[0, 0].astype(F32)
        top = _dot((hb + pos_ref[kv, 0:1, :]).astype(BF16), w1_ref[kv, 0:half, :])
        bot = _dot((hb + pos_ref[kv, 1:2, :]).astype(BF16), w1_ref[kv, half:, :])
        hid = _silu(top + pltpu.roll(bot, nh - 1, axis=0))
        out = _dot(hid.astype(BF16), w2_ref[kv])
        if kv == 0:
            out = _head_norm(out, kg_ref[...])
        out = jnp.where(row < nh - 1, out, 0.0)
        if kv == 0:
            o_ref[0, 0, :, 0:NSA_DK] = out.astype(BF16)
            o_ref[0, 0, :, NSA_DK:] = jnp.zeros((nh, LANES - NSA_DK), BF16)
        else:
            o_ref[0, 0] = out.T.astype(BF16)


def _nsa_compress_call(hk, hv, pos, w1, w2, kgain0):
    nb, ng, nh, hw = hk.shape
    hid = w1.shape[2]
    blk = pl.BlockSpec((1, 1, nh, hw), lambda b, g: (b, g, 0, 0))
    out = pl.BlockSpec((1, 1, nh, LANES), lambda b, g: (b, g, 0, 0))
    return pl.pallas_call(
        _nsa_compress_kernel,
        grid=(nb, ng),
        in_specs=[blk, blk,
                  pl.BlockSpec((2, 2, hw), lambda b, g: (0, 0, 0)),
                  pl.BlockSpec((2, 2 * hw, hid), lambda b, g: (0, 0, 0)),
                  pl.BlockSpec((2, hid, NSA_DK), lambda b, g: (0, 0, 0)),
                  pl.BlockSpec((1, NSA_DK), lambda b, g: (0, 0))],
        out_specs=[out, pl.BlockSpec((1, 1, NSA_DK, nh), lambda b, g: (b, g, 0, 0))],
        out_shape=[jax.ShapeDtypeStruct((nb, ng, nh, LANES), BF16),
                   jax.ShapeDtypeStruct((nb, ng, NSA_DK, nh), BF16)],
        compiler_params=_cparams("parallel", "parallel"),
        name="nsa_compress",
    )(hk, hv, pos.reshape(2, 2, hw), w1.astype(BF16), w2.astype(BF16), kgain0.reshape(1, NSA_DK))


def _softmax_cols(s):
    m = jnp.max(s, axis=0, keepdims=True)
    m = jnp.where(m > -jnp.inf, m, 0.0)
    e = jnp.exp2(s - m)
    d = jnp.sum(e, axis=0, keepdims=True)
    return e, 1.0 / jnp.where(d > 0, d, 1.0)


def _query_lanes(slopes_ref, g, q0, tq):
    r = NSA_REP * tq
    lane = lax.broadcasted_iota(jnp.int32, (1, r), 1)
    rep = jnp.zeros((1, r), jnp.int32)
    for i in range(1, NSA_REP):
        rep = rep + (lane >= i * tq).astype(jnp.int32)
    slope = jnp.zeros((1, r), F32)
    for i in range(NSA_REP):
        slope = jnp.where(rep == i, slopes_ref[g * NSA_REP + i], slope)
    return q0 + (lane - rep * tq), slope


def _nsa_select_kernel(slopes_ref, q_ref, kc_ref, vct_ref, mt_ref, psel_ref, qs_ref, oc_ref):
    tq = q_ref.shape[3]
    r = NSA_REP * tq
    nc = kc_ref.shape[2]
    nblk = mt_ref.shape[0]
    g = pl.program_id(1)
    q0 = pl.program_id(2) * tq
    q2 = q_ref[0, 0].reshape(r, LANES)
    t_q, slope = _query_lanes(slopes_ref, g, q0, tq)

    s_c = lax.dot_general(kc_ref[0, 0], q2, _NT, preferred_element_type=F32)
    cend = lax.broadcasted_iota(jnp.int32, (nc, r), 0) * CMP_STRIDE + (CMP_BLOCK - 1)
    dist_c = t_q - cend
    e_c, inv_c = _softmax_cols(jnp.where(dist_c >= 0, s_c - slope * dist_c.astype(F32), -jnp.inf))
    o_c = _dot(vct_ref[0, 0], e_c.astype(BF16)) * inv_c
    for i in range(NSA_REP):
        oc_ref[0, 0, i] = o_c[:, i * tq:(i + 1) * tq].astype(oc_ref.dtype)

    p_sum = e_c[:, 0:tq] * inv_c[:, 0:tq]
    for i in range(1, NSA_REP):
        p_sum = p_sum + e_c[:, i * tq:(i + 1) * tq] * inv_c[:, i * tq:(i + 1) * tq]
    mt = mt_ref[...]
    imp = sum(_dot(mt, piece) for piece in _split3(p_sum))
    blk = lax.broadcasted_iota(jnp.int32, (nblk, 1), 0).astype(F32)
    t_row = q0 + lax.broadcasted_iota(jnp.int32, (1, tq), 1)
    cur = jnp.right_shift(t_row, int(math.log2(SLC_BLOCK))).astype(F32)
    forced = (blk == 0.0) | (blk == cur) | (blk == cur - 1.0)
    valid = blk <= cur
    score = jnp.where(valid, imp + FORCE_BONUS * forced.astype(F32), -jnp.inf)
    sel = jnp.zeros((nblk, tq), F32)
    for _ in range(min(N_SELECT, nblk)):
        best = jnp.max(score, axis=0, keepdims=True)
        first = jnp.min(jnp.where(score == best, blk, float(nblk)), axis=0, keepdims=True)
        pick = blk == first
        sel = jnp.where(pick, 1.0, sel)
        score = jnp.where(pick, -jnp.inf, score)
    sel_bias = jnp.where((sel > 0.0) & valid, 0.0, -MASK_BIG).astype(BF16)
    bias_q = lax.dot_general(sel_bias, psel_ref[...], _TN, preferred_element_type=F32)
    bias_q = bias_q.astype(BF16)
    for i in range(NSA_REP):
        qs_ref[0, 0, i] = q_ref[0, 0, i] + bias_q


def _nsa_attn_kernel(slopes_ref, q_ref, qs_ref, oc_ref, ksa_ref, kwp_ref, vst_ref, vwt_ref, gt_ref, wb_ref,
                     o_ref, sa0_ref, sa1_ref, sp0_ref, sp1_ref):
    tq = q_ref.shape[3]
    r = NSA_REP * tq
    seq = ksa_ref.shape[2]
    g = pl.program_id(1)
    q0 = pl.program_id(2) * tq
    q2 = q_ref[0, 0].reshape(r, LANES)
    q_sel = qs_ref[0, 0].reshape(r, LANES)
    t_q, slope = _query_lanes(slopes_ref, g, q0, tq)

    tks = min(NSA_TKS, seq)
    pos0 = lax.broadcasted_iota(jnp.int32, (tks, r), 0)
    sb = slope * pos0.astype(F32)

    def sel_scores(j):
        k0 = pl.multiple_of(j * tks, tks)
        return lax.dot_general(ksa_ref[0, 0, pl.ds(k0, tks), :], q_sel, _NT, preferred_element_type=F32) + sb

    def sel_values(j, p):
        return _dot(vst_ref[0, 0, :, pl.ds(pl.multiple_of(j * tks, tks), tks)], p)

    def sel_softmax(j, a, m, l):
        c = slope * (j * tks + jnp.zeros((1, r), jnp.int32)).astype(F32)
        m_new = jnp.maximum(m, jnp.max(a, axis=0, keepdims=True) + c)
        m_safe = jnp.where(m_new > -jnp.inf, m_new, 0.0)
        p = jnp.exp2(a - (m_safe - c))
        alpha = jnp.exp2(m - m_safe)
        return p.astype(BF16), alpha, m_new, alpha * l + jnp.sum(p, axis=0, keepdims=True)

    def sel_step(j, a_cur, a_next, p_prev, p_cur, carry):
        m, l, acc = carry
        a_next[...] = sel_scores(j + 1)
        acc = acc + sel_values(jnp.maximum(j - 1, 0), p_prev[...])
        p, alpha, m, l = sel_softmax(j, a_cur[...], m, l)
        p_cur[...] = p
        return m, l, alpha * acc

    def sel_pair(i, carry):
        carry = sel_step(2 * i, sa0_ref, sa1_ref, sp1_ref, sp0_ref, carry)
        return sel_step(2 * i + 1, sa1_ref, sa0_ref, sp0_ref, sp1_ref, carry)

    def sel_finish(a_ref, p_ref, carry):
        m, l, acc = carry
        acc = acc + sel_values(jnp.maximum(n_big - 1, 0), p_ref[...])
        a_d = jnp.where(n_big * tks + pos0 <= t_q, a_ref[...], -jnp.inf)
        p_d, alpha_d, _, l = sel_softmax(n_big, a_d, m, l)
        acc = alpha_d * acc + sel_values(n_big, p_d)
        return acc * (1.0 / jnp.where(l > 0, l, 1.0))

    n_big = q0 // tks
    sa0_ref[...] = sel_scores(0)
    sp1_ref[...] = jnp.zeros((tks, r), BF16)
    carry = (jnp.full((1, r), -jnp.inf, F32), jnp.zeros((1, r), F32), jnp.zeros((NSA_DK, r), F32))
    carry = lax.fori_loop(0, n_big // 2, sel_pair, carry)
    o_s = lax.cond(n_big % 2 == 1,
                   lambda cr: sel_finish(sa1_ref, sp0_ref, sel_step(n_big - 1, sa0_ref, sa1_ref, sp1_ref, sp0_ref, cr)),
                   lambda cr: sel_finish(sa0_ref, sp1_ref, cr), carry)

    span = WINDOW + tq

    def window(start, bias_fn):
        s_w = lax.dot_general(kwp_ref[0, 0, pl.ds(start, span), :], q2, _NT, preferred_element_type=F32)
        e_w, inv_w = _softmax_cols(bias_fn(s_w))
        return _dot(vwt_ref[0, 0, :, pl.ds(start, span)], e_w.astype(BF16)) * inv_w

    def window_head(s_w):
        dist_w = t_q - lax.broadcasted_iota(jnp.int32, (span, r), 0)
        return jnp.where((dist_w >= 0) & (dist_w < WINDOW), s_w - slope * dist_w.astype(F32), -jnp.inf)

    o_w = lax.cond(q0 >= WINDOW,
                   lambda: window(pl.multiple_of(q0 - WINDOW, tq), lambda s_w: s_w + wb_ref[0]),
                   lambda: window(0, window_head))

    gt = gt_ref[0, 0]
    outs = []
    for i in range(NSA_REP):
        ls = slice(i * tq, (i + 1) * tq)
        outs.append(gt[3 * i:3 * i + 1, :] * oc_ref[0, 0, i].astype(F32) + gt[3 * i + 1:3 * i + 2, :] * o_s[:, ls]
                    + gt[3 * i + 2:3 * i + 3, :] * o_w[:, ls])
    o_ref[0] = jnp.concatenate(outs, axis=0).T.astype(o_ref.dtype)


def _alibi_slopes(n):
    return np.asarray([2.0 ** (-8.0 * (i + 1) / n) for i in range(n)], dtype=np.float32)


def _cmp_to_slc_t(s, nc_pad):
    n_cmp = (s - CMP_BLOCK) // CMP_STRIDE + 1
    cs = np.arange(n_cmp) * CMP_STRIDE
    ss = np.arange(s // SLC_BLOCK) * SLC_BLOCK
    ov = np.clip(np.minimum(cs[:, None] + CMP_BLOCK, ss[None, :] + SLC_BLOCK)
                 - np.maximum(cs[:, None], ss[None, :]), 0, None)
    m = np.zeros((nc_pad, s // SLC_BLOCK), np.float32)
    m[:n_cmp] = ov / CMP_STRIDE
    return m.T


def _nsa_select_call(q2, kc, vct):
    nb, ng, nrep, s, _ = q2.shape
    tq = min(NSA_TSEL, s)
    nc = kc.shape[2]
    nblk = s // SLC_BLOCK
    mt = jnp.asarray(_cmp_to_slc_t(s, nc), BF16)
    psel = np.zeros((nblk, LANES), np.float32)
    psel[np.arange(nblk), NSA_DK + np.arange(nblk)] = 1.0
    slopes = _alibi_slopes(NSA_HEADS).astype(np.float64) * LOG2E
    q_spec = pl.BlockSpec((1, 1, nrep, tq, LANES), lambda b, g, i: (b, g, 0, i, 0))
    return pl.pallas_call(
        _nsa_select_kernel,
        grid=(nb, ng, s // tq),
        in_specs=[pl.BlockSpec(memory_space=pltpu.SMEM), q_spec,
                  pl.BlockSpec((1, 1, nc, LANES), lambda b, g, i: (b, g, 0, 0)),
                  pl.BlockSpec((1, 1, NSA_DK, nc), lambda b, g, i: (b, g, 0, 0)),
                  pl.BlockSpec((nblk, nc), lambda b, g, i: (0, 0)),
                  pl.BlockSpec((nblk, LANES), lambda b, g, i: (0, 0))],
        out_specs=[q_spec, pl.BlockSpec((1, 1, nrep, NSA_DK, tq), lambda b, g, i: (b, g, 0, 0, i))],
        out_shape=[jax.ShapeDtypeStruct(q2.shape, BF16),
                   jax.ShapeDtypeStruct((nb, ng, nrep, NSA_DK, s), BF16)],
        compiler_params=_cparams("parallel", "parallel", "arbitrary"),
        name="nsa_select",
    )(jnp.asarray(slopes, F32), q2, kc, vct, mt, jnp.asarray(psel, BF16))


def _nsa_attn_call(q2, q_sel, oc, ksa, kwp, vst, vwt, gt):
    nb, ng, nrep, s, _ = q2.shape
    tq = min(NSA_TQ, s)
    tks = min(NSA_TKS, s)
    assert s >= WINDOW + tq and WINDOW % tq == 0
    slopes = _alibi_slopes(NSA_HEADS).astype(np.float64) * LOG2E
    dist = WINDOW + np.arange(tq)[None, :] - np.arange(WINDOW + tq)[:, None]
    wbias = np.where((dist >= 0) & (dist < WINDOW), -slopes[:, None, None] * dist[None], -np.inf)
    wbias = wbias.reshape(ng, nrep, WINDOW + tq, tq).transpose(0, 2, 1, 3).reshape(ng, WINDOW + tq, nrep * tq)
    seq = pl.BlockSpec((1, 1, s, LANES), lambda b, g, i: (b, g, 0, 0))
    seq_t = pl.BlockSpec((1, 1, NSA_DK, s), lambda b, g, i: (b, g, 0, 0))
    q_spec = pl.BlockSpec((1, 1, nrep, tq, LANES), lambda b, g, i: (b, g, 0, i, 0))
    return pl.pallas_call(
        _nsa_attn_kernel,
        grid=(nb, ng, s // tq),
        in_specs=[pl.BlockSpec(memory_space=pltpu.SMEM), q_spec, q_spec,
                  pl.BlockSpec((1, 1, nrep, NSA_DK, tq), lambda b, g, i: (b, g, 0, 0, i)),
                  seq, seq, seq_t, seq_t,
                  pl.BlockSpec((1, 1, gt.shape[2], tq), lambda b, g, i: (b, g, 0, i)),
                  pl.BlockSpec((1, WINDOW + tq, nrep * tq), lambda b, g, i: (g, 0, 0))],
        out_specs=pl.BlockSpec((1, tq, nrep * NSA_DK), lambda b, g, i: (b, i, g)),
        out_shape=jax.ShapeDtypeStruct((nb, s, ng * nrep * NSA_DK), BF16),
        scratch_shapes=[pltpu.VMEM((tks, nrep * tq), F32), pltpu.VMEM((tks, nrep * tq), F32),
                        pltpu.VMEM((tks, nrep * tq), BF16), pltpu.VMEM((tks, nrep * tq), BF16)],
        compiler_params=_cparams("parallel", "parallel", "arbitrary"),
        name="nsa_attention",
    )(jnp.asarray(slopes, F32), q2, q_sel, oc, ksa, kwp, vst, vwt, gt, jnp.asarray(wbias, F32))


def _out_proj_kernel(a0_ref, a1_ref, a2_ref, w0_ref, w1_ref, w2_ref, x_ref, g_ref, o_ref):
    acc = _dot(a0_ref[...], w0_ref[...]) + _dot(a1_ref[...], w1_ref[...]) + _dot(a2_ref[...], w2_ref[...])
    o_ref[...] = x_ref[...] + g_ref[0] * acc


def _out_proj_call(pool_o, gla_o, nsa_o, w_out, x2d, gate, seq, tm=1024, tn=1024):
    m, d = x2d.shape
    tm = min(tm, seq)
    k0, k1, k2 = pool_o.shape[1], gla_o.shape[1], nsa_o.shape[1]
    assert k0 == k1 and k2 == 2 * k0
    per_b = seq // tm
    a = lambda k: pl.BlockSpec((tm, k), lambda j, i: (i, 0))
    return pl.pallas_call(
        _out_proj_kernel,
        grid=(d // tn, m // tm),
        in_specs=[a(k0), a(k1), a(k2),
                  pl.BlockSpec((k0, tn), lambda j, i: (0, j)),
                  pl.BlockSpec((k1, tn), lambda j, i: (1, j)),
                  pl.BlockSpec((k2, tn), lambda j, i: (1, j)),
                  pl.BlockSpec((tm, tn), lambda j, i: (i, j)),
                  pl.BlockSpec((1, 1, tn), lambda j, i: (i // per_b, 0, j))],
        out_specs=pl.BlockSpec((tm, tn), lambda j, i: (i, j)),
        out_shape=jax.ShapeDtypeStruct((m, d), F32),
        compiler_params=_cparams("parallel", "parallel"),
        name="out_proj_residual",
    )(pool_o, gla_o, nsa_o, w_out, w_out, w_out, x2d, gate.reshape(-1, 1, d))


def _ffn_kernel(x_ref, gn_ref, sh_ref, sc_ref, wg_ref, wu_ref, wo_ref, g_ref, o_ref, acc_ref, h_ref):
    j = pl.program_id(1)

    @pl.when(j == 0)
    def _():
        acc_ref[...] = jnp.zeros_like(acc_ref)
        h_ref[...] = _norm_mod(x_ref[...], gn_ref[...], sh_ref[0], sc_ref[0])

    h = h_ref[...]
    act = (_silu(_dot(h, wg_ref[...])) * _dot(h, wu_ref[...])).astype(BF16)
    acc_ref[...] += _dot(act, wo_ref[...])

    @pl.when(j == pl.num_programs(1) - 1)
    def _():
        o_ref[...] = x_ref[...] + g_ref[0] * acc_ref[...]


def _ffn_call(x2d, gain, shift, scale, w_in, w_out, gate, seq, tm=512, th=512):
    m, d = x2d.shape
    tm = min(tm, seq)
    hid = w_out.shape[0]
    nh = hid // th
    per_b = seq // tm
    per_batch = pl.BlockSpec((1, 1, d), lambda i, j: (i // per_b, 0, 0))
    return pl.pallas_call(
        _ffn_kernel,
        grid=(m // tm, nh),
        in_specs=[pl.BlockSpec((tm, d), lambda i, j: (i, 0)),
                  pl.BlockSpec((1, d), lambda i, j: (0, 0)), per_batch, per_batch,
                  pl.BlockSpec((d, th), lambda i, j: (0, j)),
                  pl.BlockSpec((d, th), lambda i, j: (0, j + nh)),
                  pl.BlockSpec((th, d), lambda i, j: (j, 0)),
                  per_batch],
        out_specs=pl.BlockSpec((tm, d), lambda i, j: (i, 0)),
        out_shape=jax.ShapeDtypeStruct((m, d), F32),
        scratch_shapes=[pltpu.VMEM((tm, d), F32), pltpu.VMEM((tm, d), BF16)],
        compiler_params=_cparams("parallel", "arbitrary"),
        name="ffn_swiglu_residual",
    )(x2d, gain.reshape(1, d), shift.reshape(-1, 1, d), scale.reshape(-1, 1, d), w_in, w_in, w_out,
      gate.reshape(-1, 1, d))


def _regroup_w_in(w):
    d = w.shape[0]
    a = 4 * POOL_GROUP + 2 * GLA_HEADS * GLA_DK + 2 * GLA_HEADS * GLA_DV
    b = a + GLA_RANK
    c = b + NSA_HEADS * NSA_DK + 6 * NSA_GROUPS * NSA_DK
    ng = 3 * NSA_HEADS
    main = jnp.concatenate([w[:, :a], w[:, b:c]], axis=1).astype(BF16)
    small = jnp.concatenate([w[:, a:b], w[:, c:c + ng], jnp.zeros((d, LANES - GLA_RANK - ng), w.dtype)],
                            axis=1).astype(BF16)
    return main, small


def _layer(x, mod, p):
    nb, s, d = x.shape
    sh_a, sc_a, g_a, sh_f, sc_f, g_f = [mod[:, i * d:(i + 1) * d] for i in range(6)]

    x2d = x.reshape(nb * s, d)
    w_main, w_small = _regroup_w_in(p["w_in"])
    proj, small = _in_proj_call(x2d, p["norm_mix_gain"], sh_a, sc_a, w_main, w_small, s)
    proj = proj.reshape(nb, s, -1)
    small = small[0].reshape(nb, s, LANES)

    pool_o = _pool_call(proj, p["pool_w"], p["pool_scale"])
    gla_o = _gla_call(proj, small, p["gla_w_gk_up"], p["gla_b_gk"], p["gla_norm_gain"])

    q2, ksa, kwp, vst, vwt, gt = _nsa_prep_call(proj, small, p["nsa_q_gain"], p["nsa_k_gain"])
    kv0 = 4 * POOL_GROUP + 2 * GLA_HEADS * (GLA_DK + GLA_DV) + NSA_HEADS * NSA_DK
    kvw = NSA_GROUPS * NSA_DK

    def half_blocks(cols):
        t = cols.reshape(nb, s // CMP_STRIDE, CMP_STRIDE, NSA_GROUPS, NSA_DK)
        return jnp.transpose(t, (0, 3, 1, 2, 4)).reshape(nb, NSA_GROUPS, s // CMP_STRIDE, CMP_STRIDE * NSA_DK)

    kc, vc = _nsa_compress_call(half_blocks(proj[:, :, kv0:kv0 + kvw]),
                                half_blocks(proj[:, :, kv0 + kvw:kv0 + 2 * kvw]),
                                p["nsa_cmp_pos"], p["nsa_cmp_w1"], p["nsa_cmp_w2"], p["nsa_k_gain"][0])
    q_sel, oc = _nsa_select_call(q2, kc, vc)
    nsa_o = _nsa_attn_call(q2, q_sel, oc, ksa, kwp, vst, vwt, gt)

    x1 = _out_proj_call(pool_o.reshape(nb * s, -1), gla_o.reshape(nb * s, -1), nsa_o.reshape(nb * s, -1),
                        p["w_out"].astype(BF16), x2d, g_a, s)
    x2 = _ffn_call(x1, p["norm_ffn_gain"], sh_f, sc_f, p["w_ffn_in"].astype(BF16), p["w_ffn_out"].astype(BF16),
                   g_f, s)
    return x2.reshape(nb, s, d)


_PER_LAYER = ("norm_mix_gain", "norm_ffn_gain", "w_in", "w_out", "pool_w", "pool_scale", "gla_w_gk_up",
              "gla_b_gk", "gla_norm_gain", "nsa_q_gain", "nsa_k_gain", "nsa_cmp_pos", "nsa_cmp_w1",
              "nsa_cmp_w2", "w_ffn_in", "w_ffn_out")


def kernel(x, c, norm_mix_gain, norm_ffn_gain, w_mod, b_mod, w_in, w_out, pool_w, pool_scale, gla_w_gk_up,
           gla_b_gk, gla_norm_gain, nsa_q_gain, nsa_k_gain, nsa_cmp_pos, nsa_cmp_w1, nsa_cmp_w2, w_ffn_in,
           w_ffn_out):
    params = dict(norm_mix_gain=norm_mix_gain, norm_ffn_gain=norm_ffn_gain, w_in=w_in, w_out=w_out,
                  pool_w=pool_w, pool_scale=pool_scale, gla_w_gk_up=gla_w_gk_up, gla_b_gk=gla_b_gk,
                  gla_norm_gain=gla_norm_gain, nsa_q_gain=nsa_q_gain, nsa_k_gain=nsa_k_gain,
                  nsa_cmp_pos=nsa_cmp_pos, nsa_cmp_w1=nsa_cmp_w1, nsa_cmp_w2=nsa_cmp_w2,
                  w_ffn_in=w_ffn_in, w_ffn_out=w_ffn_out)
    mod = _mod_call(c, w_mod, b_mod)
    for l in range(w_mod.shape[0]):
        x = _layer(x, mod[l], {k: params[k][l] for k in _PER_LAYER})
    return x
```
